```python
import math
import jax, jax.numpy as jnp
from jax import lax
import numpy as np

D_MODEL = 1024
BATCH = 4
SEQ = 4096
DEPTH = 1
DEC_BATCH = 32
DEC_SEQ = 1
PAST_LEN = 16384
PAGE_SIZE = 128

A_HEADS = 4
A_DH = 64
A_HD = 2 * A_DH
A_WIDTH = A_HEADS * A_HD
B_HEADS = 4
B_DK = 128
B_DV = 128
B_QK = B_HEADS * B_DK
B_WIDTH = B_HEADS * B_DV
B_CONV = 2 * B_QK + B_WIDTH
CONV_W = 4
CHUNK = 64
Q_BLOCK = 128
ROPE_THETA = 10000.0
EPS = 1e-6
IN_COLS = 4 * A_WIDTH + 2 * B_QK + 2 * B_WIDTH + 2 * B_HEADS + 2 * D_MODEL

kernel_name = "hybrid_diffattn_gdeltanet_step"


def _split_points():
    sizes = (A_WIDTH, A_WIDTH, A_WIDTH, A_WIDTH, B_QK, B_QK, B_WIDTH, B_WIDTH,
             B_HEADS, B_HEADS, D_MODEL, D_MODEL)
    pts, acc = [], 0
    for s in sizes[:-1]:
        acc += s
        pts.append(acc)
    return pts


def _rms(x, g):
    xf = x.astype(jnp.float32)
    y = xf * lax.rsqrt(jnp.mean(xf * xf, axis=-1, keepdims=True) + EPS)
    return (y * g.astype(jnp.float32)).astype(x.dtype)


def _l2(x):
    return x * lax.rsqrt(jnp.sum(x * x, axis=-1, keepdims=True) + EPS)


def _rope(x, pos):
    half = A_DH // 2
    inv = ROPE_THETA ** (-jnp.arange(half, dtype=jnp.float32) / half)
    ang = pos.astype(jnp.float32)[:, None] * inv[None, :]
    cos = jnp.cos(ang)[None, :, None, None, :]
    sin = jnp.sin(ang)[None, :, None, None, :]
    xf = x.astype(jnp.float32)
    x1, x2 = xf[..., :half], xf[..., half:]
    return jnp.concatenate([x1 * cos - x2 * sin, x2 * cos + x1 * sin], -1).astype(x.dtype)


def _diff_attend(q, k, v, q_pos, k_pos, lam):
    s = jnp.einsum('nqhmd,nkhmd->nhmqk', q, k).astype(jnp.float32) * (A_DH ** -0.5)
    mask = k_pos[None, :] <= q_pos[:, None]
    p = jax.nn.softmax(jnp.where(mask, s, -jnp.inf), axis=-1)
    p = p[:, :, 0] - lam * p[:, :, 1]
    return jnp.einsum('nhqk,nkhe->nqhe', p.astype(v.dtype), v)


def _short_conv(u, buf, w):
    t = u.shape[1]
    full = jnp.concatenate([buf.astype(u.dtype), u], axis=1)
    y = full[:, 0:t] * w[0]
    for i in range(1, CONV_W):
        y = y + full[:, i:i + t] * w[i]
    return jax.nn.silu(y), full[:, -(CONV_W - 1):]


def _gdn_chunked(q, k, v, beta, g, s0):
    n, t, h, _ = q.shape
    dv = v.shape[-1]
    nc = t // CHUNK

    def blk(a):
        return a.reshape(n, nc, CHUNK, h, a.shape[-1]).transpose(1, 0, 3, 2, 4)

    q, k, v = blk(q), blk(k), blk(v)
    beta = beta.reshape(n, nc, CHUNK, h).transpose(1, 0, 3, 2)
    gc = jnp.cumsum(g.reshape(n, nc, CHUNK, h).transpose(1, 0, 3, 2), axis=-1)
    idx = jnp.arange(CHUNK)
    incl = idx[:, None] >= idx[None, :]
    strict = idx[:, None] > idx[None, :]
    decay = jnp.exp(jnp.where(incl, gc[..., :, None] - gc[..., None, :], -jnp.inf))
    kb = k * beta[..., None]
    lmat = jnp.where(strict, jnp.einsum('...id,...jd->...ij', kb, k) * decay, 0.0)
    amat = lmat + jnp.eye(CHUNK, dtype=lmat.dtype)
    rhs = jnp.concatenate([v * beta[..., None], kb * jnp.exp(gc)[..., None]], axis=-1)
    sol = lax.linalg.triangular_solve(amat, rhs, left_side=True, lower=True,
                                      unit_diagonal=True)
    u, w = sol[..., :dv], sol[..., dv:]
    qk = jnp.where(incl, jnp.einsum('...id,...jd->...ij', q, k) * decay, 0.0)
    qg = q * jnp.exp(gc)[..., None]
    kg = k * jnp.exp(gc[..., -1:] - gc)[..., None]
    glast = jnp.exp(gc[..., -1])

    def step(s, xs):
        qg_c, kg_c, u_c, w_c, qk_c, gl_c = xs
        v_new = u_c - jnp.einsum('nhcd,nhde->nhce', w_c, s)
        o = (jnp.einsum('nhcd,nhde->nhce', qg_c, s)
             + jnp.einsum('nhij,nhje->nhie', qk_c, v_new))
        s = s * gl_c[..., None, None] + jnp.einsum('nhcd,nhce->nhde', kg_c, v_new)
        return s, o

    s, o = lax.scan(step, s0, (qg, kg, u, w, qk, glast))
    return o.transpose(1, 0, 3, 2, 4).reshape(n, t, h, dv), s


def _gdn_recurrent(q, k, v, beta, g, s0):
    def step(s, xs):
        qt, kt, vt, bt, gt = xs
        s = s * jnp.exp(gt)[..., None, None]
        ut = bt[..., None] * (vt - jnp.einsum('nhde,nhd->nhe', s, kt))
        s = s + jnp.einsum('nhd,nhe->nhde', kt, ut)
        return s, jnp.einsum('nhde,nhd->nhe', s, qt)

    xs = (q.swapaxes(0, 1), k.swapaxes(0, 1), v.swapaxes(0, 1),
          beta.swapaxes(0, 1), g.swapaxes(0, 1))
    s, o = lax.scan(step, s0, xs)
    return o.swapaxes(0, 1), s


def _layer(x, pos, past_k, past_v, conv_buf, s0, lam_init,
           w_norm, w_in, a_qn, a_kn, a_lq1, a_lk1, a_lq2, a_lk2, a_subln,
           conv_w, a_log, dt_bias, b_onorm, w_up_a, w_up_b, w_out):
    n, t, _ = x.shape
    h = _rms(x, w_norm)
    proj = h @ w_in
    aq, ak, av, az, bq, bk, bv, bz, bb, ba, ga, gb = jnp.split(proj, _split_points(), axis=-1)

    aq = _rope(_rms(aq.reshape(n, t, A_HEADS, 2, A_DH), a_qn), pos)
    ak = _rope(_rms(ak.reshape(n, t, A_HEADS, 2, A_DH), a_kn), pos)
    av = av.reshape(n, t, A_HEADS, A_HD)
    f32 = jnp.float32
    lam = (jnp.exp(jnp.sum(a_lq1.astype(f32) * a_lk1.astype(f32)))
           - jnp.exp(jnp.sum(a_lq2.astype(f32) * a_lk2.astype(f32))) + lam_init)
    if past_k is None:
        nb = t // Q_BLOCK
        qb = aq.reshape(n, nb, Q_BLOCK, A_HEADS, 2, A_DH).swapaxes(0, 1)
        pb = pos.reshape(nb, Q_BLOCK)
        ob = lax.map(lambda a: _diff_attend(a[0], ak, av, a[1], pos, lam), (qb, pb))
        o_a = ob.swapaxes(0, 1).reshape(n, t, A_HEADS, A_HD)
    else:
        p_len = past_k.shape[1]
        k_all = jnp.concatenate([past_k.reshape(n, p_len, A_HEADS, 2, A_DH).astype(ak.dtype), ak], axis=1)
        v_all = jnp.concatenate([past_v.astype(av.dtype), av], axis=1)
        k_pos = jnp.arange(p_len + t, dtype=jnp.int32)
        o_a = _diff_attend(aq, k_all, v_all, pos, k_pos, lam)
    o_a = _rms(o_a, a_subln) * (1.0 - lam_init)
    o_a = o_a.reshape(n, t, A_WIDTH) * jax.nn.silu(az)

    if conv_buf is None:
        conv_buf = jnp.zeros((n, CONV_W - 1, B_CONV), x.dtype)
    qkv, new_buf = _short_conv(jnp.concatenate([bq, bk, bv], axis=-1), conv_buf, conv_w)
    bq, bk, bv = jnp.split(qkv, [B_QK, 2 * B_QK], axis=-1)
    bq = _l2(bq.reshape(n, t, B_HEADS, B_DK).astype(f32)) * (B_DK ** -0.5)
    bk = _l2(bk.reshape(n, t, B_HEADS, B_DK).astype(f32))
    bv = bv.reshape(n, t, B_HEADS, B_DV).astype(f32)
    beta = jax.nn.sigmoid(bb.astype(f32))
    gdec = -jnp.exp(a_log.astype(f32)) * jax.nn.softplus(ba.astype(f32) + dt_bias.astype(f32))
    if s0 is None:
        o_b, s_new = _gdn_chunked(bq, bk, bv, beta, gdec,
                                  jnp.zeros((n, B_HEADS, B_DK, B_DV), f32))
    else:
        o_b, s_new = _gdn_recurrent(bq, bk, bv, beta, gdec, s0.astype(f32))
    o_b = _rms(o_b.astype(x.dtype), b_onorm) * jax.nn.silu(bz.reshape(n, t, B_HEADS, B_DV))
    o_b = o_b.reshape(n, t, B_WIDTH)

    y = jax.nn.sigmoid(ga) * (o_a @ w_up_a) + jax.nn.sigmoid(gb) * (o_b @ w_up_b)
    out = x + y @ w_out
    new_k = ak.reshape(n, t, A_HEADS, A_HD)
    return out, new_k, av, new_buf, s_new.astype(x.dtype)


def setup_inputs(seed: int = 0) -> dict:
    key = jax.random.key(seed)
    ks = jax.random.split(key, 32)
    n_pages = PAST_LEN // PAGE_SIZE
    n_pool = (DEC_BATCH * n_pages * 5) // 4
    f32 = jnp.float32
    nrm = lambda k, s, sc: jax.random.normal(k, s, f32) * sc
    page_table = jax.random.permutation(ks[0], n_pool)[:DEC_BATCH * n_pages]
    page_table = page_table.reshape(DEC_BATCH, n_pages).astype(jnp.int32)
    dt = jnp.exp(jax.random.uniform(ks[17], (DEPTH, B_HEADS), f32,
                                    math.log(1e-3), math.log(1e-1)))
    return {
        "x_prompt": nrm(ks[1], (BATCH, SEQ, D_MODEL), 1.0),
        "x_sample": nrm(ks[2], (DEC_BATCH, DEC_SEQ, D_MODEL), 1.0),
        "cache_k": nrm(ks[3], (DEPTH, n_pool, PAGE_SIZE, A_HEADS, A_HD), 1.0),
        "cache_v": nrm(ks[4], (DEPTH, n_pool, PAGE_SIZE, A_HEADS, A_HD), 1.0),
        "state_conv": nrm(ks[5], (DEPTH, DEC_BATCH, CONV_W - 1, B_CONV), 1.0),
        "state_ssm": nrm(ks[6], (DEPTH, DEC_BATCH, B_HEADS, B_DK, B_DV), 0.1),
        "page_table": page_table,
        "w_norm": 1.0 + nrm(ks[7], (DEPTH, D_MODEL), 0.02),
        "w_in": nrm(ks[8], (DEPTH, D_MODEL, IN_COLS), D_MODEL ** -0.5),
        "a_qn": 1.0 + nrm(ks[9], (DEPTH, A_DH), 0.02),
        "a_kn": 1.0 + nrm(ks[10], (DEPTH, A_DH), 0.02),
        "a_lq1": nrm(ks[11], (DEPTH, A_DH), 0.1),
        "a_lk1": nrm(ks[12], (DEPTH, A_DH), 0.1),
        "a_lq2": nrm(ks[13], (DEPTH, A_DH), 0.1),
        "a_lk2": nrm(ks[14], (DEPTH, A_DH), 0.1),
        "a_subln": 1.0 + nrm(ks[15], (DEPTH, A_HD), 0.02),
        "conv_w": nrm(ks[16], (DEPTH, CONV_W, B_CONV), CONV_W ** -0.5),
        "a_log": jnp.log(jax.random.uniform(ks[18], (DEPTH, B_HEADS), f32, 1.0, 16.0)),
        "dt_bias": dt + jnp.log(-jnp.expm1(-dt)),
        "b_onorm": 1.0 + nrm(ks[19], (DEPTH, B_DV), 0.02),
        "w_up_a": nrm(ks[20], (DEPTH, A_WIDTH, D_MODEL), A_WIDTH ** -0.5),
        "w_up_b": nrm(ks[21], (DEPTH, B_WIDTH, D_MODEL), B_WIDTH ** -0.5),
        "w_out": nrm(ks[22], (DEPTH, D_MODEL, D_MODEL), D_MODEL ** -0.5),
    }


def reference(x_prompt, x_sample, cache_k, cache_v, state_conv, state_ssm, page_table,
              w_norm, w_in, a_qn, a_kn, a_lq1, a_lk1, a_lq2, a_lk2, a_subln,
              conv_w, a_log, dt_bias, b_onorm, w_up_a, w_up_b, w_out):
    n_dec, n_pages = page_table.shape
    past_len = n_pages * PAGE_SIZE
    pos_p = jnp.arange(x_prompt.shape[1], dtype=jnp.int32)
    pos_s = past_len + jnp.arange(x_sample.shape[1], dtype=jnp.int32)
    hp, hs = x_prompt, x_sample
    kp, vp, cp, sp, ksl, vsl, csl, ssl = [], [], [], [], [], [], [], []
    for l in range(DEPTH):
        lam_init = 0.8 - 0.6 * math.exp(-0.3 * l)
        w = (w_norm[l], w_in[l], a_qn[l], a_kn[l], a_lq1[l], a_lk1[l], a_lq2[l], a_lk2[l],
             a_subln[l], conv_w[l], a_log[l], dt_bias[l], b_onorm[l], w_up_a[l], w_up_b[l], w_out[l])
        hp, k1, v1, c1, s1 = _layer(hp, pos_p, None, None, None, None, lam_init, *w)
        past_k = cache_k[l][page_table].reshape(n_dec, past_len, A_HEADS, A_HD)
        past_v = cache_v[l][page_table].reshape(n_dec, past_len, A_HEADS, A_HD)
        hs, k2, v2, c2, s2 = _layer(hs, pos_s, past_k, past_v, state_conv[l], state_ssm[l],
                                    lam_init, *w)
        kp.append(k1); vp.append(v1); cp.append(c1); sp.append(s1)
        ksl.append(k2); vsl.append(v2); csl.append(c2); ssl.append(s2)
    return (hp, hs, jnp.stack(kp), jnp.stack(vp), jnp.stack(cp), jnp.stack(sp),
            jnp.stack(ksl), jnp.stack(vsl), jnp.stack(csl), jnp.stack(ssl))
```

```python
import functools
import math

import jax
import jax.numpy as jnp
from jax import lax
from jax.experimental import pallas as pl
from jax.experimental.pallas import tpu as pltpu

F32 = jnp.float32
BF16 = jnp.bfloat16

A_HEADS = 4
A_DH = 64
A_HD = 2 * A_DH
A_WIDTH = A_HEADS * A_HD
B_HEADS = 4
B_DK = 128
B_DV = 128
B_QK = B_HEADS * B_DK
B_WIDTH = B_HEADS * B_DV
B_CONV = 2 * B_QK + B_WIDTH
CONV_W = 4
CHUNK = 64
PAGE_SIZE = 128
ROPE_THETA = 10000.0
EPS = 1e-6

LANES = 128
SUBLANES = 8
VMEM_LIMIT_BYTES = 56 * 1024 * 1024

PROJ_ROWS = 256
ATTN_BLOCK = 256
GDN_ROWS = 1024
MAX_PAGES_PER_STEP = 8
GDN_STEP_SEQS = 4

PROJ_COLS = 2 * A_WIDTH + A_WIDTH + B_CONV + LANES
GATE_COLS = A_WIDTH + B_WIDTH + 2 * 1024


def _dot(a, b):
    return jnp.dot(a, b, preferred_element_type=F32)


def _dot_nt(a, b):
    return lax.dot_general(a, b, (((1,), (1,)), ((), ())), preferred_element_type=F32)


def _dot_tn(a, b):
    return lax.dot_general(a, b, (((0,), (0,)), ((), ())), preferred_element_type=F32)


def _split3(x):
    x1 = x.astype(BF16)
    r1 = x - x1.astype(F32)
    x2 = r1.astype(BF16)
    x3 = (r1 - x2.astype(F32)).astype(BF16)
    return x1, x2, x3


def _dot_exact_lhs(mat01, x):
    x1, x2, x3 = _split3(x)
    return _dot(mat01, x1) + _dot(mat01, x2) + _dot(mat01, x3)


def _dot_exact_rhs(x, mat01):
    x1, x2, x3 = _split3(x)
    return _dot(x1, mat01) + _dot(x2, mat01) + _dot(x3, mat01)


def _sigmoid(x):
    return 1.0 / (1.0 + jnp.exp(-x))


def _softplus(x):
    return jnp.maximum(x, 0.0) + jnp.log1p(jnp.exp(-jnp.abs(x)))


def _rms_rows(x, gain):
    return x * lax.rsqrt(jnp.mean(x * x, axis=-1, keepdims=True) + EPS) * gain


def _project_norm_rope(hb, w_ref, col0, gsum_ref, gain, cos, sin, out_ref, scale):
    a = _dot(hb, w_ref[:, col0:col0 + A_WIDTH])
    ms = _dot((a * a).astype(BF16), gsum_ref[...])
    y = a * lax.rsqrt(ms + EPS) * gain
    lane = lax.broadcasted_iota(jnp.int32, (a.shape[0], LANES), 1)
    first_half = (lane % A_DH) < (A_DH // 2)
    for j in range(A_WIDTH // LANES):
        t = y[:, j * LANES:(j + 1) * LANES]
        partner = jnp.where(first_half,
                            pltpu.roll(t, LANES - A_DH // 2, 1),
                            pltpu.roll(t, A_DH // 2, 1))
        r = t * cos + partner * sin
        out_ref[:, j * LANES:(j + 1) * LANES] = (r * scale).astype(out_ref.dtype)


def _gdn_qkv_epilogue(y, gq_ref, gk_ref, gv_ref):
    act = y * _sigmoid(y)
    for h in range(B_HEADS):
        sl = slice(h * B_DK, (h + 1) * B_DK)
        qh = act[:, h * B_DK:(h + 1) * B_DK]
        kh = act[:, B_QK + h * B_DK:B_QK + (h + 1) * B_DK]
        gq_ref[:, sl] = qh * lax.rsqrt(jnp.sum(qh * qh, axis=-1, keepdims=True) + EPS) * (B_DK ** -0.5)
        gk_ref[:, sl] = kh * lax.rsqrt(jnp.sum(kh * kh, axis=-1, keepdims=True) + EPS)
    gv_ref[...] = act[:, 2 * B_QK:]


def _beta_g(z, alog_row, dtb_row):
    lane = lax.broadcasted_iota(jnp.int32, z.shape, 1)
    beta = _sigmoid(z)
    g = -jnp.exp(alog_row) * _softplus(z + dtb_row)
    return jnp.where(lane < B_HEADS, beta, jnp.where(lane < 2 * B_HEADS, g, 0.0))


def _prompt_proj_kernel(x_ref, wn_ref, w_ref, gsum_ref, qn_ref, kn_ref, cos_ref, sin_ref,
                        convw_ref, alog_ref, dtb_ref,
                        q_ref, k_ref, v_ref, gq_ref, gk_ref, gv_ref, bg_ref, conv_ref,
                        ext_ref, *, tiles_per_seq):
    i = pl.program_id(0)
    tm = x_ref.shape[0]
    hb = _rms_rows(x_ref[...], wn_ref[...]).astype(BF16)
    cos = cos_ref[...]
    sin = sin_ref[...]
    _project_norm_rope(hb, w_ref, 0, gsum_ref, qn_ref[...], cos, sin, q_ref, A_DH ** -0.5)
    _project_norm_rope(hb, w_ref, A_WIDTH, gsum_ref, kn_ref[...], cos, sin, k_ref, 1.0)
    v_ref[...] = _dot(hb, w_ref[:, 2 * A_WIDTH:3 * A_WIDTH])

    u = _dot(hb, w_ref[:, 3 * A_WIDTH:3 * A_WIDTH + B_CONV])

    @pl.when(i % tiles_per_seq == 0)
    def _():
        ext_ref[0:SUBLANES, :] = jnp.zeros((SUBLANES, B_CONV), F32)

    ext_ref[SUBLANES:SUBLANES + tm, :] = u
    w = convw_ref[...]
    y = (ext_ref[SUBLANES - 3:SUBLANES - 3 + tm, :] * w[0:1]
         + ext_ref[SUBLANES - 2:SUBLANES - 2 + tm, :] * w[1:2]
         + ext_ref[SUBLANES - 1:SUBLANES - 1 + tm, :] * w[2:3]
         + u * w[3:4])
    conv_ref[0] = ext_ref[tm + SUBLANES - (CONV_W - 1):tm + SUBLANES, :]
    ext_ref[0:SUBLANES, :] = ext_ref[tm:tm + SUBLANES, :]
    _gdn_qkv_epilogue(y, gq_ref, gk_ref, gv_ref)
    bg_ref[...] = _beta_g(_dot(hb, w_ref[:, 3 * A_WIDTH + B_CONV:]), alog_ref[...], dtb_ref[...])


def _sample_proj_kernel(x_ref, wn_ref, w_ref, gsum_ref, qn_ref, kn_ref, cos_ref, sin_ref,
                        convw_ref, alog_ref, dtb_ref, c0_ref, c1_ref, c2_ref,
                        q_ref, k_ref, v_ref, gq_ref, gk_ref, gv_ref, bg_ref, u_ref):
    hb = _rms_rows(x_ref[...], wn_ref[...]).astype(BF16)
    cos = cos_ref[...]
    sin = sin_ref[...]
    _project_norm_rope(hb, w_ref, 0, gsum_ref, qn_ref[...], cos, sin, q_ref, A_DH ** -0.5)
    _project_norm_rope(hb, w_ref, A_WIDTH, gsum_ref, kn_ref[...], cos, sin, k_ref, 1.0)
    v_ref[...] = _dot(hb, w_ref[:, 2 * A_WIDTH:3 * A_WIDTH])
    u = _dot(hb, w_ref[:, 3 * A_WIDTH:3 * A_WIDTH + B_CONV])
    u_ref[...] = u
    w = convw_ref[...]
    y = c0_ref[...] * w[0:1] + c1_ref[...] * w[1:2] + c2_ref[...] * w[2:3] + u * w[3:4]
    _gdn_qkv_epilogue(y, gq_ref, gk_ref, gv_ref)
    bg_ref[...] = _beta_g(_dot(hb, w_ref[:, 3 * A_WIDTH + B_CONV:]), alog_ref[...], dtb_ref[...])


def _full(shape):
    return pl.BlockSpec(shape, lambda *_: (0,) * len(shape))


def _prompt_projection(x2, n_seq, seq, wn, w_proj, gsum, qn, kn, cos, sin, convw, alog, dtb):
    rows = x2.shape[0]
    tm = min(PROJ_ROWS, seq)
    assert seq % tm == 0 and tm % 16 == 0
    tps = seq // tm
    d = x2.shape[1]
    row_spec = lambda width: pl.BlockSpec((tm, width), lambda i: (i, 0))
    pos_spec = pl.BlockSpec((tm, LANES), lambda i: (i % tps, 0))
    out_shape = (
        jax.ShapeDtypeStruct((rows, A_WIDTH), BF16),
        jax.ShapeDtypeStruct((rows, A_WIDTH), F32),
        jax.ShapeDtypeStruct((rows, A_WIDTH), F32),
        jax.ShapeDtypeStruct((rows, B_QK), F32),
        jax.ShapeDtypeStruct((rows, B_QK), F32),
        jax.ShapeDtypeStruct((rows, B_WIDTH), F32),
        jax.ShapeDtypeStruct((rows, LANES), F32),
        jax.ShapeDtypeStruct((n_seq, CONV_W - 1, B_CONV), F32),
    )
    return pl.pallas_call(
        functools.partial(_prompt_proj_kernel, tiles_per_seq=tps),
        grid=(rows // tm,),
        in_specs=[row_spec(d), _full(wn.shape), _full(w_proj.shape), _full(gsum.shape),
                  _full(qn.shape), _full(kn.shape), pos_spec, pos_spec,
                  _full(convw.shape), _full(alog.shape), _full(dtb.shape)],
        out_specs=(row_spec(A_WIDTH), row_spec(A_WIDTH), row_spec(A_WIDTH),
                   row_spec(B_QK), row_spec(B_QK), row_spec(B_WIDTH), row_spec(LANES),
                   pl.BlockSpec((1, CONV_W - 1, B_CONV), lambda i: (i // tps, 0, 0))),
        out_shape=out_shape,
        scratch_shapes=[pltpu.VMEM((tm + SUBLANES, B_CONV), F32)],
        compiler_params=pltpu.CompilerParams(dimension_semantics=("arbitrary",),
                                             vmem_limit_bytes=VMEM_LIMIT_BYTES),
        name="prompt_projection",
    )(x2, wn, w_proj, gsum, qn, kn, cos, sin, convw, alog, dtb)


def _sample_projection(x2, wn, w_proj, gsum, qn, kn, cos, sin, convw, alog, dtb, c0, c1, c2):
    rows = x2.shape[0]
    out_shape = (
        jax.ShapeDtypeStruct((rows, A_WIDTH), F32),
        jax.ShapeDtypeStruct((rows, A_WIDTH), F32),
        jax.ShapeDtypeStruct((rows, A_WIDTH), F32),
        jax.ShapeDtypeStruct((rows, B_QK), F32),
        jax.ShapeDtypeStruct((rows, B_QK), F32),
        jax.ShapeDtypeStruct((rows, B_WIDTH), F32),
        jax.ShapeDtypeStruct((rows, LANES), F32),
        jax.ShapeDtypeStruct((rows, B_CONV), F32),
    )
    args = (x2, wn, w_proj, gsum, qn, kn, cos, sin, convw, alog, dtb, c0, c1, c2)
    return pl.pallas_call(
        _sample_proj_kernel,
        grid=(1,),
        in_specs=[_full(a.shape) for a in args],
        out_specs=tuple(_full(s.shape) for s in out_shape),
        out_shape=out_shape,
        compiler_params=pltpu.CompilerParams(dimension_semantics=("arbitrary",),
                                             vmem_limit_bytes=VMEM_LIMIT_BYTES),
        name="sample_projection",
    )(*args)


def _lambda_value(lamp, lam_init):
    s1 = jnp.sum(lamp[0:1, :] * lamp[1:2, :], axis=1, keepdims=True)
    s2 = jnp.sum(lamp[2:3, :] * lamp[3:4, :], axis=1, keepdims=True)
    return jnp.exp(s1) - jnp.exp(s2) + lam_init


def _prompt_attn_kernel(lamp_ref, subln_ref, q_ref, k_ref, v_ref, o_ref, kb_ref, vb_ref, *, lam_init):
    qb = pl.program_id(2)
    tq = q_ref.shape[1]

    @pl.when(qb == 0)
    def _():
        kb_ref[...] = k_ref[0].astype(BF16)
        vb_ref[...] = v_ref[0].astype(BF16)

    q = q_ref[0]
    lane = lax.broadcasted_iota(jnp.int32, q.shape, 1)
    zero = jnp.zeros_like(q)
    qq = jnp.concatenate([jnp.where(lane < A_DH, q, zero), jnp.where(lane >= A_DH, q, zero)], axis=0)

    def step(kv, carry, masked):
        m, l, acc = carry
        start = pl.multiple_of(kv * tq, tq)
        kt = kb_ref[pl.ds(start, tq), :]
        vt = vb_ref[pl.ds(start, tq), :]
        s = _dot_nt(qq, kt)
        if masked:
            row = lax.broadcasted_iota(jnp.int32, s.shape, 0) % tq
            col = lax.broadcasted_iota(jnp.int32, s.shape, 1)
            s = jnp.where(col <= row, s, -jnp.inf)
        m_new = jnp.maximum(m, jnp.max(s, axis=1, keepdims=True))
        alpha = jnp.exp(m - m_new)
        p = jnp.exp(s - m_new)
        l = alpha * l + jnp.sum(p, axis=1, keepdims=True)
        acc = alpha * acc + _dot(p.astype(BF16), vt)
        return m_new, l, acc

    init = (jnp.full((2 * tq, 1), -jnp.inf, F32), jnp.zeros((2 * tq, 1), F32),
            jnp.zeros((2 * tq, A_HD), F32))
    carry = lax.fori_loop(0, qb, lambda kv, c: step(kv, c, False), init)
    _, l, acc = step(qb, carry, True)
    o = acc / l
    lam = _lambda_value(lamp_ref[...], lam_init)
    out = o[:tq] - lam * o[tq:]
    o_ref[0] = _rms_rows(out, subln_ref[...]) * (1.0 - lam_init)


def _prompt_attention(q, k, v, lamp, subln, n_seq, seq, lam_init):
    tq = min(ATTN_BLOCK, seq)
    assert seq % tq == 0
    q3 = q.reshape(n_seq, seq, A_WIDTH)
    k3 = k.reshape(n_seq, seq, A_WIDTH)
    v3 = v.reshape(n_seq, seq, A_WIDTH)
    kv_spec = pl.BlockSpec((1, seq, A_HD), lambda n, h, i: (n, 0, h))
    blk_spec = pl.BlockSpec((1, tq, A_HD), lambda n, h, i: (n, i, h))
    out = pl.pallas_call(
        functools.partial(_prompt_attn_kernel, lam_init=lam_init),
        grid=(n_seq, A_HEADS, seq // tq),
        in_specs=[_full(lamp.shape), _full(subln.shape), blk_spec, kv_spec, kv_spec],
        out_specs=blk_spec,
        out_shape=jax.ShapeDtypeStruct((n_seq, seq, A_WIDTH), F32),
        scratch_shapes=[pltpu.VMEM((seq, A_HD), BF16), pltpu.VMEM((seq, A_HD), BF16)],
        compiler_params=pltpu.CompilerParams(
            dimension_semantics=("arbitrary", "arbitrary", "arbitrary"),
            vmem_limit_bytes=VMEM_LIMIT_BYTES),
        name="prompt_attention",
    )(lamp, subln, q3, k3, v3)
    return out.reshape(n_seq * seq, A_WIDTH)


def _paged_attn_kernel(pt_ref, lamp_ref, subln_ref, q_ref, kn_ref, vn_ref, *rest,
                       pages_per_step, lam_init):
    del pt_ref
    k_refs = rest[:pages_per_step]
    v_refs = rest[pages_per_step:2 * pages_per_step]
    o_ref, m_ref, l_ref, acc_ref = rest[2 * pages_per_step:]
    step = pl.program_id(1)
    rows = 2 * A_HEADS
    row = lax.broadcasted_iota(jnp.int32, (rows, A_WIDTH), 0)
    lane = lax.broadcasted_iota(jnp.int32, (rows, A_WIDTH), 1)
    qmat = jnp.where(lane // A_DH == row, q_ref[0], 0.0)

    @pl.when(step == 0)
    def _():
        m_ref[...] = jnp.sum(qmat * kn_ref[0], axis=1, keepdims=True)
        l_ref[...] = jnp.ones_like(l_ref)
        acc_ref[...] = jnp.broadcast_to(vn_ref[0], acc_ref.shape)

    qb = qmat.astype(BF16)
    s = jnp.concatenate([_dot_nt(qb, k_refs[j][0].astype(BF16)) for j in range(pages_per_step)], axis=1)
    m_prev = m_ref[...]
    m_new = jnp.maximum(m_prev, jnp.max(s, axis=1, keepdims=True))
    alpha = jnp.exp(m_prev - m_new)
    p = jnp.exp(s - m_new)
    l_ref[...] = alpha * l_ref[...] + jnp.sum(p, axis=1, keepdims=True)
    pb = p.astype(BF16)
    pv = _dot(pb[:, 0:PAGE_SIZE], v_refs[0][0].astype(BF16))
    for j in range(1, pages_per_step):
        pv = pv + _dot(pb[:, j * PAGE_SIZE:(j + 1) * PAGE_SIZE], v_refs[j][0].astype(BF16))
    acc_ref[...] = alpha * acc_ref[...] + pv
    m_ref[...] = m_new

    @pl.when(step == pl.num_programs(1) - 1)
    def _():
        o = acc_ref[...] / l_ref[...]
        lam = _lambda_value(lamp_ref[...], lam_init)
        head = lane // A_HD
        comb = jnp.where(row == 2 * head, o, 0.0) - lam * jnp.where(row == 2 * head + 1, o, 0.0)
        out = jnp.sum(comb, axis=0, keepdims=True)
        for h in range(A_HEADS):
            sl = slice(h * A_HD, (h + 1) * A_HD)
            o_ref[0, :, sl] = _rms_rows(out[:, sl], subln_ref[...]) * (1.0 - lam_init)


def _paged_attention(q, k_new, v_new, cache_k, cache_v, page_table, lamp, subln, lam_init):
    n_dec, n_pages = page_table.shape
    n_pool = cache_k.shape[0]
    pps = max(p for p in range(1, MAX_PAGES_PER_STEP + 1) if n_pages % p == 0)
    ck = cache_k.reshape(n_pool, PAGE_SIZE, A_WIDTH)
    cv = cache_v.reshape(n_pool, PAGE_SIZE, A_WIDTH)
    q3 = q.reshape(n_dec, 1, A_WIDTH)
    kn3 = k_new.reshape(n_dec, 1, A_WIDTH)
    vn3 = v_new.reshape(n_dec, 1, A_WIDTH)
    row_spec = pl.BlockSpec((1, 1, A_WIDTH), lambda b, s, pt: (b, 0, 0))

    def page_spec(j):
        return pl.BlockSpec((1, PAGE_SIZE, A_WIDTH),
                            lambda b, s, pt: (pt[b * n_pages + s * pps + j], 0, 0))

    const = lambda shape: pl.BlockSpec(shape, lambda b, s, pt: (0,) * len(shape))
    grid_spec = pltpu.PrefetchScalarGridSpec(
        num_scalar_prefetch=1,
        grid=(n_dec, n_pages // pps),
        in_specs=[const(lamp.shape), const(subln.shape), row_spec, row_spec, row_spec]
                 + [page_spec(j) for j in range(pps)] + [page_spec(j) for j in range(pps)],
        out_specs=row_spec,
        scratch_shapes=[pltpu.VMEM((2 * A_HEADS, 1), F32), pltpu.VMEM((2 * A_HEADS, 1), F32),
                        pltpu.VMEM((2 * A_HEADS, A_WIDTH), F32)],
    )
    out = pl.pallas_call(
        functools.partial(_paged_attn_kernel, pages_per_step=pps, lam_init=lam_init),
        grid_spec=grid_spec,
        out_shape=jax.ShapeDtypeStruct((n_dec, 1, A_WIDTH), F32),
        compiler_params=pltpu.CompilerParams(dimension_semantics=("arbitrary", "arbitrary"),
                                             vmem_limit_bytes=VMEM_LIMIT_BYTES),
        name="paged_attention",
    )(page_table.reshape(-1), lamp, subln, q3, kn3, vn3, *([ck] * pps), *([cv] * pps))
    return out.reshape(n_dec, A_WIDTH)


def _gdn_chunk_kernel(gq_ref, gk_ref, gv_ref, bg_ref, onorm_ref, o_ref, s_out_ref,
                      s_ref, beta_ref, g_ref, u_ref, w_ref, qg_ref, kg_ref, qk_ref, gl_ref):
    tb = pl.program_id(1)
    tt = gq_ref.shape[0]
    nc = tt // CHUNK

    @pl.when(tb == 0)
    def _():
        s_ref[...] = jnp.zeros_like(s_ref)

    r = lax.broadcasted_iota(jnp.int32, (LANES, B_QK), 0)
    c = lax.broadcasted_iota(jnp.int32, (LANES, B_QK), 1)
    bg = bg_ref[...]
    beta_ref[...] = _dot_exact_rhs(bg, (r == c // B_DK).astype(BF16))
    g_ref[...] = _dot_exact_rhs(bg, (r == B_HEADS + c // B_DK).astype(BF16))

    ii = lax.broadcasted_iota(jnp.int32, (CHUNK, CHUNK), 0)
    jj = lax.broadcasted_iota(jnp.int32, (CHUNK, CHUNK), 1)
    incl = ii >= jj
    strict = ii > jj
    ltri = incl.astype(BF16)

    def phase1(ci, carry):
        r0 = pl.multiple_of(ci * CHUNK, CHUNK)
        g = g_ref[pl.ds(r0, CHUNK), :]
        beta = beta_ref[pl.ds(r0, CHUNK), :]
        gc = _dot_exact_lhs(ltri, g)
        glast = gc[CHUNK - 1:CHUNK, :]
        eg = jnp.exp(gc)
        ekg = jnp.exp(glast - gc)
        gl_ref[ci] = jnp.exp(glast)
        for h in range(B_HEADS):
            sl = slice(h * B_DK, (h + 1) * B_DK)
            q = gq_ref[pl.ds(r0, CHUNK), sl]
            k = gk_ref[pl.ds(r0, CHUNK), sl]
            v = gv_ref[pl.ds(r0, CHUNK), sl]
            bh = beta[:, sl]
            kb = k * bh
            d = _dot_exact_lhs(ltri, jnp.where(strict, g[:, h * B_DK:h * B_DK + CHUNK], 0.0))
            decay = jnp.exp(jnp.where(incl, d, -jnp.inf))
            kbf = k.astype(BF16)
            m = -jnp.where(strict, _dot_nt(kb.astype(BF16), kbf) * decay, 0.0)
            sol = jnp.concatenate([v * bh, kb * eg[:, sl]], axis=1)
            sol = sol + _dot(m.astype(BF16), sol.astype(BF16))
            p = m
            for _ in range(5):
                pb = p.astype(BF16)
                p = _dot(pb, pb)
                sol = sol + _dot(p.astype(BF16), sol.astype(BF16))
            u_ref[ci, h] = sol[:, :B_DV]
            w_ref[ci, h] = sol[:, B_DV:].astype(BF16)
            qk = jnp.where(incl, _dot_nt(q.astype(BF16), kbf) * decay, 0.0)
            qk_ref[ci, h] = qk.astype(BF16)
            qg_ref[ci, h] = (q * eg[:, sl]).astype(BF16)
            kg_ref[ci, h] = (k * ekg[:, sl]).astype(BF16)
        return carry

    lax.fori_loop(0, nc, phase1, 0)

    onorm = onorm_ref[...]

    def phase2(ci, carry):
        r0 = pl.multiple_of(ci * CHUNK, CHUNK)
        gl = gl_ref[ci]
        for h in range(B_HEADS):
            sl = slice(h * B_DV, (h + 1) * B_DV)
            s = s_ref[h]
            sb = s.astype(BF16)
            vn = u_ref[ci, h] - _dot(w_ref[ci, h], sb)
            vnb = vn.astype(BF16)
            o = _dot(qg_ref[ci, h], sb) + _dot(qk_ref[ci, h], vnb)
            s_ref[h] = s * gl[:, sl] + _dot_tn(kg_ref[ci, h], vnb)
            o_ref[pl.ds(r0, CHUNK), sl] = _rms_rows(o, onorm)
        return carry

    lax.fori_loop(0, nc, phase2, 0)

    @pl.when(tb == pl.num_programs(1) - 1)
    def _():
        s_out_ref[0] = s_ref[...]


def _gdn_chunked(gq, gk, gv, bg, onorm, n_seq, seq):
    tt = min(GDN_ROWS, seq)
    assert seq % tt == 0 and tt % CHUNK == 0
    ntb = seq // tt
    nc = tt // CHUNK
    row_spec = lambda width: pl.BlockSpec((tt, width), lambda n, t: (n * ntb + t, 0))
    out_shape = (jax.ShapeDtypeStruct((n_seq * seq, B_WIDTH), F32),
                 jax.ShapeDtypeStruct((n_seq, B_HEADS, B_DK, B_DV), F32))
    return pl.pallas_call(
        _gdn_chunk_kernel,
        grid=(n_seq, ntb),
        in_specs=[row_spec(B_QK), row_spec(B_QK), row_spec(B_WIDTH), row_spec(LANES),
                  pl.BlockSpec(onorm.shape, lambda n, t: (0, 0))],
        out_specs=(row_spec(B_WIDTH),
                   pl.BlockSpec((1, B_HEADS, B_DK, B_DV), lambda n, t: (n, 0, 0, 0))),
        out_shape=out_shape,
        scratch_shapes=[
            pltpu.VMEM((B_HEADS, B_DK, B_DV), F32),
            pltpu.VMEM((tt, B_QK), F32),
            pltpu.VMEM((tt, B_QK), F32),
            pltpu.VMEM((nc, B_HEADS, CHUNK, B_DV), F32),
            pltpu.VMEM((nc, B_HEADS, CHUNK, B_DK), BF16),
            pltpu.VMEM((nc, B_HEADS, CHUNK, B_DK), BF16),
            pltpu.VMEM((nc, B_HEADS, CHUNK, B_DK), BF16),
            pltpu.VMEM((nc, B_HEADS, CHUNK, CHUNK), BF16),
            pltpu.VMEM((nc, 1, B_QK), F32),
        ],
        compiler_params=pltpu.CompilerParams(dimension_semantics=("arbitrary", "arbitrary"),
                                             vmem_limit_bytes=VMEM_LIMIT_BYTES),
        name="gdn_chunked",
    )(gq, gk, gv, bg, onorm)


def _gdn_step_kernel(gq_ref, gk_ref, gv_ref, bg_ref, onorm_ref, s_ref, o_ref, so_ref):
    ri = lax.broadcasted_iota(jnp.int32, (B_DK, B_DK), 0)
    ci = lax.broadcasted_iota(jnp.int32, (B_DK, B_DK), 1)
    eye = (ri == ci).astype(F32)
    onorm = onorm_ref[...]
    for b in range(gq_ref.shape[0]):
        bg = bg_ref[b]
        for h in range(B_HEADS):
            sl = slice(h * B_DK, (h + 1) * B_DK)
            q = gq_ref[b][:, sl]
            k = gk_ref[b][:, sl]
            v = gv_ref[b][:, sl]
            beta = bg[:, h:h + 1]
            a = jnp.exp(bg[:, B_HEADS + h:B_HEADS + h + 1])
            kcol = jnp.sum(eye * k, axis=1, keepdims=True)
            qcol = jnp.sum(eye * q, axis=1, keepdims=True)
            s = s_ref[b, h] * a
            u = beta * (v - jnp.sum(s * kcol, axis=0, keepdims=True))
            s = s + kcol * u
            so_ref[b, h] = s
            o = jnp.sum(s * qcol, axis=0, keepdims=True)
            o_ref[b, :, sl] = _rms_rows(o, onorm)


def _gdn_step(gq, gk, gv, bg, onorm, state):
    n_dec = state.shape[0]
    bb = max(p for p in range(1, GDN_STEP_SEQS + 1) if n_dec % p == 0)
    row_spec = lambda width: pl.BlockSpec((bb, 1, width), lambda i: (i, 0, 0))
    s_spec = pl.BlockSpec((bb, B_HEADS, B_DK, B_DV), lambda i: (i, 0, 0, 0))
    out_shape = (jax.ShapeDtypeStruct((n_dec, 1, B_WIDTH), F32),
                 jax.ShapeDtypeStruct(state.shape, F32))
    o, s_new = pl.pallas_call(
        _gdn_step_kernel,
        grid=(n_dec // bb,),
        in_specs=[row_spec(B_QK), row_spec(B_QK), row_spec(B_WIDTH), row_spec(LANES),
                  pl.BlockSpec(onorm.shape, lambda i: (0, 0)), s_spec],
        out_specs=(row_spec(B_WIDTH), s_spec),
        out_shape=out_shape,
        compiler_params=pltpu.CompilerParams(dimension_semantics=("arbitrary",),
                                             vmem_limit_bytes=VMEM_LIMIT_BYTES),
        name="gdn_step",
    )(gq.reshape(n_dec, 1, B_QK), gk.reshape(n_dec, 1, B_QK), gv.reshape(n_dec, 1, B_WIDTH),
      bg.reshape(n_dec, 1, LANES), onorm, state.astype(F32))
    return o.reshape(n_dec, B_WIDTH), s_new


def _merge_out_kernel(x_ref, wn_ref, wg_ref, oa_ref, ob_ref, wua_ref, wub_ref, wo_ref, y_ref):
    x = x_ref[...]
    hb = _rms_rows(x, wn_ref[...]).astype(BF16)
    az = _dot(hb, wg_ref[:, 0:A_WIDTH])
    bz = _dot(hb, wg_ref[:, A_WIDTH:A_WIDTH + B_WIDTH])
    oa = oa_ref[...] * (az * _sigmoid(az))
    ob = ob_ref[...] * (bz * _sigmoid(bz))
    d = x.shape[1]
    g0 = A_WIDTH + B_WIDTH
    ga = _dot(hb, wg_ref[:, g0:g0 + d])
    gb = _dot(hb, wg_ref[:, g0 + d:g0 + 2 * d])
    y = (_sigmoid(ga) * _dot(oa.astype(BF16), wua_ref[...])
         + _sigmoid(gb) * _dot(ob.astype(BF16), wub_ref[...]))
    y_ref[...] = x + _dot(y.astype(BF16), wo_ref[...])


def _merge_out(x2, wn, w_gate, o_a, o_b, w_up_a, w_up_b, w_out, tm):
    rows, d = x2.shape
    assert rows % tm == 0
    row_spec = lambda width: pl.BlockSpec((tm, width), lambda i: (i, 0))
    return pl.pallas_call(
        _merge_out_kernel,
        grid=(rows // tm,),
        in_specs=[row_spec(d), _full(wn.shape), _full(w_gate.shape), row_spec(A_WIDTH), row_spec(B_WIDTH),
                  _full(w_up_a.shape), _full(w_up_b.shape), _full(w_out.shape)],
        out_specs=row_spec(d),
        out_shape=jax.ShapeDtypeStruct((rows, d), F32),
        compiler_params=pltpu.CompilerParams(dimension_semantics=("arbitrary",),
                                             vmem_limit_bytes=VMEM_LIMIT_BYTES),
        name="merge_out",
    )(x2, wn, w_gate, o_a, o_b, w_up_a, w_up_b, w_out)


def _rope_tables(pos):
    half = A_DH // 2
    inv = ROPE_THETA ** (-jnp.arange(half, dtype=F32) / half)
    ang = pos.astype(F32)[:, None] * inv[None, :]
    cos = jnp.cos(ang)
    sin = jnp.sin(ang)
    return jnp.tile(cos, (1, 4)), jnp.concatenate([-sin, sin, -sin, sin], axis=1)


def _repack_w_in(w):
    d = w.shape[0]
    o = 0
    parts = {}
    for name, size in (("aq", A_WIDTH), ("ak", A_WIDTH), ("av", A_WIDTH), ("az", A_WIDTH),
                       ("bq", B_QK), ("bk", B_QK), ("bv", B_WIDTH), ("bz", B_WIDTH),
                       ("bb", B_HEADS), ("ba", B_HEADS), ("ga", d), ("gb", d)):
        parts[name] = w[:, o:o + size]
        o += size
    assert o == w.shape[1]
    pad = jnp.zeros((d, LANES - 2 * B_HEADS), w.dtype)
    w_proj = jnp.concatenate([parts[n] for n in ("aq", "ak", "av", "bq", "bk", "bv", "bb", "ba")] + [pad], axis=1)
    w_gate = jnp.concatenate([parts[n] for n in ("az", "bz", "ga", "gb")], axis=1)
    return w_proj.astype(BF16), w_gate.astype(BF16)


def _lane_row(vec, offset):
    return jnp.zeros((1, LANES), F32).at[0, offset:offset + vec.shape[0]].set(vec.astype(F32))


def kernel(x_prompt, x_sample, cache_k, cache_v, state_conv, state_ssm, page_table, w_norm, w_in, a_qn, a_kn, a_lq1, a_lk1, a_lq2, a_lk2, a_subln, conv_w, a_log, dt_bias, b_onorm, w_up_a, w_up_b, w_out):
    depth = w_norm.shape[0]
    n_seq, seq, d = x_prompt.shape
    n_dec, dec_seq, _ = x_sample.shape
    assert dec_seq == 1, "the sample path handles one new token per sequence"
    n_pages = page_table.shape[1]
    past_len = n_pages * PAGE_SIZE

    cos_p, sin_p = _rope_tables(jnp.arange(seq, dtype=jnp.int32))
    cos_s, sin_s = _rope_tables(past_len + jnp.arange(dec_seq, dtype=jnp.int32))
    sub = lax.broadcasted_iota(jnp.int32, (A_WIDTH, A_WIDTH), 0) // A_DH
    gsum = jnp.where(sub == sub.T, 1.0 / A_DH, 0.0).astype(BF16)

    hp = x_prompt.reshape(n_seq * seq, d)
    hs = x_sample.reshape(n_dec, d)
    outs = [[] for _ in range(8)]
    for l in range(depth):
        lam_init = 0.8 - 0.6 * math.exp(-0.3 * l)
        wn = w_norm[l].reshape(1, d)
        w_proj, w_gate = _repack_w_in(w_in[l])
        qn = jnp.tile(a_qn[l], A_WIDTH // A_DH).reshape(1, A_WIDTH)
        kn = jnp.tile(a_kn[l], A_WIDTH // A_DH).reshape(1, A_WIDTH)
        lamp = jnp.stack([a_lq1[l], a_lk1[l], a_lq2[l], a_lk2[l]]).astype(F32)
        subln = a_subln[l].reshape(1, A_HD)
        alog = _lane_row(a_log[l], B_HEADS)
        dtb = _lane_row(dt_bias[l], B_HEADS)
        onorm = b_onorm[l].reshape(1, B_DV)
        wua = w_up_a[l].astype(BF16)
        wub = w_up_b[l].astype(BF16)
        wo = w_out[l].astype(BF16)
        convw = conv_w[l]

        q, k, v, gq, gk, gv, bg, conv_p = _prompt_projection(
            hp, n_seq, seq, wn, w_proj, gsum, qn, kn, cos_p, sin_p, convw, alog, dtb)
        o_a = _prompt_attention(q, k, v, lamp, subln, n_seq, seq, lam_init)
        o_b, ssm_p = _gdn_chunked(gq, gk, gv, bg, onorm, n_seq, seq)
        hp = _merge_out(hp, wn, w_gate, o_a, o_b, wua, wub, wo, min(PROJ_ROWS, seq))

        sc = state_conv[l]
        qs, ks, vs, gqs, gks, gvs, bgs, us = _sample_projection(
            hs, wn, w_proj, gsum, qn, kn, cos_s, sin_s, convw, alog, dtb,
            sc[:, 0, :], sc[:, 1, :], sc[:, 2, :])
        o_as = _paged_attention(qs, ks, vs, cache_k[l], cache_v[l], page_table, lamp, subln, lam_init)
        o_bs, ssm_s = _gdn_step(gqs, gks, gvs, bgs, onorm, state_ssm[l])
        hs = _merge_out(hs, wn, w_gate, o_as, o_bs, wua, wub, wo, n_dec)

        outs[0].append(k.reshape(n_seq, seq, A_HEADS, A_HD))
        outs[1].append(v.reshape(n_seq, seq, A_HEADS, A_HD))
        outs[2].append(conv_p)
        outs[3].append(ssm_p.astype(x_prompt.dtype))
        outs[4].append(ks.reshape(n_dec, dec_seq, A_HEADS, A_HD))
        outs[5].append(vs.reshape(n_dec, dec_seq, A_HEADS, A_HD))
        outs[6].append(jnp.concatenate([sc[:, 1:, :], us[:, None, :]], axis=1))
        outs[7].append(ssm_s.astype(x_sample.dtype))

    return (hp.reshape(n_seq, seq, d), hs.reshape(n_dec, dec_seq, d)) + tuple(jnp.stack(o) for o in outs)
```

```python
import functools
import math

import jax
import jax.numpy as jnp
from jax import lax
from jax.experimental import pallas as pl
from jax.experimental.pallas import tpu as pltpu

F32 = jnp.float32
BF16 = jnp.bfloat16

A_HEADS = 4
A_DH = 64
A_HD = 2 * A_DH
A_WIDTH = A_HEADS * A_HD
B_HEADS = 4
B_DK = 128
B_DV = 128
B_QK = B_HEADS * B_DK
B_WIDTH = B_HEADS * B_DV
B_CONV = 2 * B_QK + B_WIDTH
CONV_W = 4
CHUNK = 64
PAGE_SIZE = 128
ROPE_THETA = 10000.0
EPS = 1e-6

LANES = 128
SUBLANES = 8
VMEM_LIMIT_BYTES = 56 * 1024 * 1024

PROJ_ROWS = 256
ATTN_BLOCK = 512
MERGE_ROWS = 256
GDN_ROWS = 256
MAX_PAGES_PER_STEP = 16
GDN_STEP_SEQS = 4


def _dot(a, b):
    return jnp.dot(a, b, preferred_element_type=F32)


def _dot_nt(a, b):
    return lax.dot_general(a, b, (((1,), (1,)), ((), ())), preferred_element_type=F32)


def _bmm(a, b):
    return lax.dot_general(a, b, (((2,), (1,)), ((0,), (0,))), preferred_element_type=F32)


def _bmm_nt(a, b):
    return lax.dot_general(a, b, (((2,), (2,)), ((0,), (0,))), preferred_element_type=F32)


def _bmm_tn(a, b):
    return lax.dot_general(a, b, (((1,), (1,)), ((0,), (0,))), preferred_element_type=F32)


def _split3(x):
    x1 = x.astype(BF16)
    r1 = x - x1.astype(F32)
    x2 = r1.astype(BF16)
    x3 = (r1 - x2.astype(F32)).astype(BF16)
    return x1, x2, x3


def _dot_exact_lhs(mat01, x):
    x1, x2, x3 = _split3(x)
    return _dot(mat01, x1) + _dot(mat01, x2) + _dot(mat01, x3)


def _sigmoid(x):
    return 1.0 / (1.0 + jnp.exp(-x))


def _softplus(x):
    return jnp.maximum(x, 0.0) + jnp.log1p(jnp.exp(-jnp.abs(x)))


def _rms_rows(x, gain):
    return x * lax.rsqrt(jnp.mean(x * x, axis=-1, keepdims=True) + EPS) * gain


def _project_norm_rope(hb, w_ref, col0, gsum_ref, gain, cos, sin):
    a = _dot(hb, w_ref[:, col0:col0 + A_WIDTH])
    ms = _dot((a * a).astype(BF16), gsum_ref[...])
    y = a * lax.rsqrt(ms + EPS) * gain
    lane = lax.broadcasted_iota(jnp.int32, (a.shape[0], LANES), 1)
    first_half = (lane % A_DH) < (A_DH // 2)
    heads = []
    for j in range(A_HEADS):
        t = y[:, j * A_HD:(j + 1) * A_HD]
        partner = jnp.where(first_half,
                            pltpu.roll(t, LANES - A_DH // 2, 1),
                            pltpu.roll(t, A_DH // 2, 1))
        heads.append(t * cos + partner * sin)
    return heads


def _gdn_qkv_epilogue(y, gq_ref, gk_ref, gv_ref):
    act = y * _sigmoid(y)
    for h in range(B_HEADS):
        sl = slice(h * B_DK, (h + 1) * B_DK)
        qh = act[:, h * B_DK:(h + 1) * B_DK]
        kh = act[:, B_QK + h * B_DK:B_QK + (h + 1) * B_DK]
        gq_ref[:, sl] = qh * lax.rsqrt(jnp.sum(qh * qh, axis=-1, keepdims=True) + EPS) * (B_DK ** -0.5)
        gk_ref[:, sl] = kh * lax.rsqrt(jnp.sum(kh * kh, axis=-1, keepdims=True) + EPS)
    gv_ref[...] = act[:, 2 * B_QK:]


def _beta_g(z, alog_row, dtb_row):
    lane = lax.broadcasted_iota(jnp.int32, z.shape, 1)
    beta = _sigmoid(z)
    g = -jnp.exp(alog_row) * _softplus(z + dtb_row)
    return jnp.where(lane < B_HEADS, beta, jnp.where(lane < 2 * B_HEADS, g, 0.0))


def _prompt_proj_kernel(x_ref, wn_ref, w_ref, gsum_ref, qn_ref, kn_ref, cos_ref, sin_ref,
                        convw_ref, alog_ref, dtb_ref,
                        q_ref, k_ref, kb_ref, v_ref, vt_ref, gq_ref, gk_ref, gv_ref, bg_ref, conv_ref,
                        ext_ref, *, tiles_per_seq):
    i = pl.program_id(0)
    tm = x_ref.shape[0]
    hb = _rms_rows(x_ref[...], wn_ref[...]).astype(BF16)
    cos = cos_ref[...]
    sin = sin_ref[...]
    q_heads = _project_norm_rope(hb, w_ref, 0, gsum_ref, qn_ref[...], cos, sin)
    k_heads = _project_norm_rope(hb, w_ref, A_WIDTH, gsum_ref, kn_ref[...], cos, sin)
    v = _dot(hb, w_ref[:, 2 * A_WIDTH:3 * A_WIDTH])
    for h in range(A_HEADS):
        sl = slice(h * A_HD, (h + 1) * A_HD)
        q_ref[:, sl] = (q_heads[h] * (A_DH ** -0.5)).astype(BF16)
        k_ref[pl.ds(h, tm, stride=A_HEADS), :] = k_heads[h]
        kb_ref[:, sl] = k_heads[h].astype(BF16)
        v_ref[pl.ds(h, tm, stride=A_HEADS), :] = v[:, sl]
        vt_ref[0, 0, h] = v[:, sl].T.astype(BF16)

    u = _dot(hb, w_ref[:, 3 * A_WIDTH:3 * A_WIDTH + B_CONV])

    @pl.when(i % tiles_per_seq == 0)
    def _():
        ext_ref[0:SUBLANES, :] = jnp.zeros((SUBLANES, B_CONV), F32)

    ext_ref[SUBLANES:SUBLANES + tm, :] = u
    w = convw_ref[...]
    y = (ext_ref[SUBLANES - 3:SUBLANES - 3 + tm, :] * w[0:1]
         + ext_ref[SUBLANES - 2:SUBLANES - 2 + tm, :] * w[1:2]
         + ext_ref[SUBLANES - 1:SUBLANES - 1 + tm, :] * w[2:3]
         + u * w[3:4])
    conv_ref[0] = ext_ref[tm + SUBLANES - (CONV_W - 1):tm + SUBLANES, :]
    ext_ref[0:SUBLANES, :] = ext_ref[tm:tm + SUBLANES, :]
    _gdn_qkv_epilogue(y, gq_ref, gk_ref, gv_ref)
    bg_ref[...] = _beta_g(_dot(hb, w_ref[:, 3 * A_WIDTH + B_CONV:]), alog_ref[...], dtb_ref[...])


def _sample_proj_kernel(x_ref, wn_ref, w_ref, gsum_ref, qn_ref, kn_ref, cos_ref, sin_ref,
                        convw_ref, alog_ref, dtb_ref, c0_ref, c1_ref, c2_ref,
                        q_ref, k_ref, v_ref, gq_ref, gk_ref, gv_ref, bg_ref, u_ref):
    hb = _rms_rows(x_ref[...], wn_ref[...]).astype(BF16)
    cos = cos_ref[...]
    sin = sin_ref[...]
    q_heads = _project_norm_rope(hb, w_ref, 0, gsum_ref, qn_ref[...], cos, sin)
    k_heads = _project_norm_rope(hb, w_ref, A_WIDTH, gsum_ref, kn_ref[...], cos, sin)
    for h in range(A_HEADS):
        sl = slice(h * A_HD, (h + 1) * A_HD)
        q_ref[:, sl] = q_heads[h] * (A_DH ** -0.5)
        k_ref[:, sl] = k_heads[h]
    v_ref[...] = _dot(hb, w_ref[:, 2 * A_WIDTH:3 * A_WIDTH])
    u = _dot(hb, w_ref[:, 3 * A_WIDTH:3 * A_WIDTH + B_CONV])
    u_ref[...] = u
    w = convw_ref[...]
    y = c0_ref[...] * w[0:1] + c1_ref[...] * w[1:2] + c2_ref[...] * w[2:3] + u * w[3:4]
    _gdn_qkv_epilogue(y, gq_ref, gk_ref, gv_ref)
    bg_ref[...] = _beta_g(_dot(hb, w_ref[:, 3 * A_WIDTH + B_CONV:]), alog_ref[...], dtb_ref[...])


def _full(shape):
    return pl.BlockSpec(shape, lambda *_: (0,) * len(shape))


def _prompt_projection(x2, n_seq, seq, wn, w_proj, gsum, qn, kn, cos, sin, convw, alog, dtb):
    rows = x2.shape[0]
    tm = min(PROJ_ROWS, seq)
    assert seq % tm == 0 and tm % LANES == 0
    tps = seq // tm
    d = x2.shape[1]
    row_spec = lambda width: pl.BlockSpec((tm, width), lambda i: (i, 0))
    head_row_spec = pl.BlockSpec((A_HEADS * tm, A_HD), lambda i: (i, 0))
    pos_spec = pl.BlockSpec((tm, LANES), lambda i: (i % tps, 0))
    out_shape = (
        jax.ShapeDtypeStruct((rows, A_WIDTH), BF16),
        jax.ShapeDtypeStruct((rows * A_HEADS, A_HD), F32),
        jax.ShapeDtypeStruct((rows, A_WIDTH), BF16),
        jax.ShapeDtypeStruct((rows * A_HEADS, A_HD), F32),
        jax.ShapeDtypeStruct((n_seq, tps, A_HEADS, A_HD, tm), BF16),
        jax.ShapeDtypeStruct((rows, B_QK), F32),
        jax.ShapeDtypeStruct((rows, B_QK), F32),
        jax.ShapeDtypeStruct((rows, B_WIDTH), F32),
        jax.ShapeDtypeStruct((rows, LANES), F32),
        jax.ShapeDtypeStruct((n_seq, CONV_W - 1, B_CONV), F32),
    )
    return pl.pallas_call(
        functools.partial(_prompt_proj_kernel, tiles_per_seq=tps),
        grid=(rows // tm,),
        in_specs=[row_spec(d), _full(wn.shape), _full(w_proj.shape), _full(gsum.shape),
                  _full(qn.shape), _full(kn.shape), pos_spec, pos_spec,
                  _full(convw.shape), _full(alog.shape), _full(dtb.shape)],
        out_specs=(row_spec(A_WIDTH), head_row_spec, row_spec(A_WIDTH), head_row_spec,
                   pl.BlockSpec((1, 1, A_HEADS, A_HD, tm), lambda i: (i // tps, i % tps, 0, 0, 0)),
                   row_spec(B_QK), row_spec(B_QK), row_spec(B_WIDTH), row_spec(LANES),
                   pl.BlockSpec((1, CONV_W - 1, B_CONV), lambda i: (i // tps, 0, 0))),
        out_shape=out_shape,
        scratch_shapes=[pltpu.VMEM((tm + SUBLANES, B_CONV), F32)],
        compiler_params=pltpu.CompilerParams(dimension_semantics=("arbitrary",),
                                             vmem_limit_bytes=VMEM_LIMIT_BYTES),
        name="prompt_projection",
    )(x2, wn, w_proj, gsum, qn, kn, cos, sin, convw, alog, dtb)


def _sample_projection(x2, wn, w_proj, gsum, qn, kn, cos, sin, convw, alog, dtb, c0, c1, c2):
    rows = x2.shape[0]
    out_shape = (
        jax.ShapeDtypeStruct((rows, A_WIDTH), F32),
        jax.ShapeDtypeStruct((rows, A_WIDTH), F32),
        jax.ShapeDtypeStruct((rows, A_WIDTH), F32),
        jax.ShapeDtypeStruct((rows, B_QK), F32),
        jax.ShapeDtypeStruct((rows, B_QK), F32),
        jax.ShapeDtypeStruct((rows, B_WIDTH), F32),
        jax.ShapeDtypeStruct((rows, LANES), F32),
        jax.ShapeDtypeStruct((rows, B_CONV), F32),
    )
    args = (x2, wn, w_proj, gsum, qn, kn, cos, sin, convw, alog, dtb, c0, c1, c2)
    return pl.pallas_call(
        _sample_proj_kernel,
        grid=(1,),
        in_specs=[_full(a.shape) for a in args],
        out_specs=tuple(_full(s.shape) for s in out_shape),
        out_shape=out_shape,
        compiler_params=pltpu.CompilerParams(dimension_semantics=("arbitrary",),
                                             vmem_limit_bytes=VMEM_LIMIT_BYTES),
        name="sample_projection",
    )(*args)


def _lambda_value(lamp, lam_init):
    s1 = jnp.sum(lamp[0:1, :] * lamp[1:2, :], axis=1, keepdims=True)
    s2 = jnp.sum(lamp[2:3, :] * lamp[3:4, :], axis=1, keepdims=True)
    return jnp.exp(s1) - jnp.exp(s2) + lam_init


def _prompt_attn_kernel(lamp_ref, subln_ref, q_ref, k_ref, vt_ref, o_ref, *, lam_init):
    qb = pl.program_id(2)
    tq = q_ref.shape[1]
    tv = vt_ref.shape[-1]
    q = q_ref[0]
    lane = lax.broadcasted_iota(jnp.int32, q.shape, 1)
    zero = jnp.zeros_like(q)
    qq = jnp.concatenate([jnp.where(lane < A_DH, q, zero), jnp.where(lane >= A_DH, q, zero)], axis=0)

    def step(kv, carry, masked):
        m, l, acc = carry
        start = pl.multiple_of(kv * tq, tq)
        s = _dot_nt(k_ref[0, pl.ds(start, tq), :], qq)
        if masked:
            key = lax.broadcasted_iota(jnp.int32, s.shape, 0)
            qpos = lax.broadcasted_iota(jnp.int32, s.shape, 1) % tq
            s = jnp.where(key <= qpos, s, -jnp.inf)
        m_new = jnp.maximum(m, jnp.max(s, axis=0, keepdims=True))
        alpha = jnp.exp(m - m_new)
        p = jnp.exp(s - m_new)
        l = alpha * l + jnp.sum(p, axis=0, keepdims=True)
        pb = p.astype(BF16)
        acc = alpha * acc
        for j in range(tq // tv):
            acc = acc + _dot(vt_ref[0, kv * (tq // tv) + j, 0], pb[j * tv:(j + 1) * tv])
        return m_new, l, acc

    init = (jnp.full((1, 2 * tq), -jnp.inf, F32), jnp.zeros((1, 2 * tq), F32),
            jnp.zeros((A_HD, 2 * tq), F32))
    carry = lax.fori_loop(0, qb, lambda kv, c: step(kv, c, False), init)
    _, l, acc = step(qb, carry, True)
    o = acc / l
    lam = _lambda_value(lamp_ref[...], lam_init)
    out = (o[:, :tq] - lam * o[:, tq:]).T
    o_ref[0] = _rms_rows(out, subln_ref[...]) * (1.0 - lam_init)


def _prompt_attention(q, kb, vt, lamp, subln, n_seq, seq, lam_init):
    tv = vt.shape[-1]
    tq = min(ATTN_BLOCK, seq)
    assert seq % tq == 0 and tq % tv == 0
    q3 = q.reshape(n_seq, seq, A_WIDTH)
    k3 = kb.reshape(n_seq, seq, A_WIDTH)
    blk_spec = pl.BlockSpec((1, tq, A_HD), lambda n, h, i: (n, i, h))
    out = pl.pallas_call(
        functools.partial(_prompt_attn_kernel, lam_init=lam_init),
        grid=(n_seq, A_HEADS, seq // tq),
        in_specs=[_full(lamp.shape), _full(subln.shape), blk_spec,
                  pl.BlockSpec((1, seq, A_HD), lambda n, h, i: (n, 0, h)),
                  pl.BlockSpec((1, seq // tv, 1, A_HD, tv), lambda n, h, i: (n, 0, h, 0, 0))],
        out_specs=blk_spec,
        out_shape=jax.ShapeDtypeStruct((n_seq, seq, A_WIDTH), F32),
        compiler_params=pltpu.CompilerParams(
            dimension_semantics=("arbitrary", "arbitrary", "arbitrary"),
            vmem_limit_bytes=VMEM_LIMIT_BYTES),
        name="prompt_attention",
    )(lamp, subln, q3, k3, vt)
    return out.reshape(n_seq * seq, A_WIDTH)


def _paged_attn_kernel(pt_ref, lamp_ref, subln_ref, q_ref, kn_ref, vn_ref, *rest,
                       pages_per_step, lam_init):
    del pt_ref
    k_refs = rest[:pages_per_step]
    v_refs = rest[pages_per_step:2 * pages_per_step]
    o_ref, m_ref, l_ref, acc_ref = rest[2 * pages_per_step:]
    step = pl.program_id(1)
    rows = 2 * A_HEADS
    page_rows = PAGE_SIZE * A_HEADS
    qrow = lax.broadcasted_iota(jnp.int32, (rows, A_HD), 0)
    qlane = lax.broadcasted_iota(jnp.int32, (rows, A_HD), 1)
    qmat = jnp.where(qlane // A_DH == qrow // A_HEADS, q_ref[0], 0.0)

    @pl.when(step == 0)
    def _():
        m_ref[...] = jnp.sum(qmat * kn_ref[0], axis=1, keepdims=True)
        l_ref[...] = jnp.ones_like(l_ref)
        acc_ref[...] = vn_ref[0]

    qb = qmat.astype(BF16)
    s = jnp.concatenate([_dot_nt(qb, k_refs[j][0].astype(BF16)) for j in range(pages_per_step)], axis=1)
    row = lax.broadcasted_iota(jnp.int32, s.shape, 0)
    col = lax.broadcasted_iota(jnp.int32, s.shape, 1)
    s = jnp.where(col % A_HEADS == row % A_HEADS, s, -jnp.inf)
    m_prev = m_ref[...]
    m_new = jnp.maximum(m_prev, jnp.max(s, axis=1, keepdims=True))
    alpha = jnp.exp(m_prev - m_new)
    p = jnp.exp(s - m_new)
    l_ref[...] = alpha * l_ref[...] + jnp.sum(p, axis=1, keepdims=True)
    pb = p.astype(BF16)
    pv = _dot(pb[:, 0:page_rows], v_refs[0][0].astype(BF16))
    for j in range(1, pages_per_step):
        pv = pv + _dot(pb[:, j * page_rows:(j + 1) * page_rows], v_refs[j][0].astype(BF16))
    acc_ref[...] = alpha * acc_ref[...] + pv
    m_ref[...] = m_new

    @pl.when(step == pl.num_programs(1) - 1)
    def _():
        o = acc_ref[...] / l_ref[...]
        lam = _lambda_value(lamp_ref[...], lam_init)
        out = o[:A_HEADS] - lam * o[A_HEADS:]
        o_ref[0] = _rms_rows(out, subln_ref[...]) * (1.0 - lam_init)


def _paged_attention(q, k_new, v_new, cache_k, cache_v, page_table, lamp, subln, lam_init):
    n_dec, n_pages = page_table.shape
    n_pool = cache_k.shape[0]
    pps = max(p for p in range(1, MAX_PAGES_PER_STEP + 1) if n_pages % p == 0)
    ck = cache_k.reshape(n_pool, PAGE_SIZE * A_HEADS, A_HD)
    cv = cache_v.reshape(n_pool, PAGE_SIZE * A_HEADS, A_HD)
    stack2 = lambda a: jnp.tile(a.reshape(n_dec, A_HEADS, A_HD), (1, 2, 1))
    rows = 2 * A_HEADS
    row_spec = pl.BlockSpec((1, rows, A_HD), lambda b, s, pt: (b, 0, 0))

    def page_spec(j):
        return pl.BlockSpec((1, PAGE_SIZE * A_HEADS, A_HD),
                            lambda b, s, pt: (pt[b * n_pages + s * pps + j], 0, 0))

    const = lambda shape: pl.BlockSpec(shape, lambda b, s, pt: (0,) * len(shape))
    grid_spec = pltpu.PrefetchScalarGridSpec(
        num_scalar_prefetch=1,
        grid=(n_dec, n_pages // pps),
        in_specs=[const(lamp.shape), const(subln.shape), row_spec, row_spec, row_spec]
                 + [page_spec(j) for j in range(pps)] + [page_spec(j) for j in range(pps)],
        out_specs=pl.BlockSpec((1, A_HEADS, A_HD), lambda b, s, pt: (b, 0, 0)),
        scratch_shapes=[pltpu.VMEM((rows, 1), F32), pltpu.VMEM((rows, 1), F32),
                        pltpu.VMEM((rows, A_HD), F32)],
    )
    out = pl.pallas_call(
        functools.partial(_paged_attn_kernel, pages_per_step=pps, lam_init=lam_init),
        grid_spec=grid_spec,
        out_shape=jax.ShapeDtypeStruct((n_dec, A_HEADS, A_HD), F32),
        compiler_params=pltpu.CompilerParams(dimension_semantics=("arbitrary", "arbitrary"),
                                             vmem_limit_bytes=VMEM_LIMIT_BYTES),
        name="paged_attention",
    )(page_table.reshape(-1), lamp, subln, stack2(q), stack2(k_new), stack2(v_new),
      *([ck] * pps), *([cv] * pps))
    return out.reshape(n_dec, A_WIDTH)


def _gdn_chunk_kernel(gq_ref, gk_ref, gv_ref, bg_ref, onorm_ref, o_ref, s_out_ref, s_ref):
    tb = pl.program_id(0)
    n_seq, tt, _ = gq_ref.shape
    nc = tt // CHUNK

    @pl.when(tb == 0)
    def _():
        s_ref[...] = jnp.zeros_like(s_ref)

    ii = lax.broadcasted_iota(jnp.int32, (CHUNK, CHUNK), 0)
    jj = lax.broadcasted_iota(jnp.int32, (CHUNK, CHUNK), 1)
    incl = ii >= jj
    strict = ii > jj
    ltri = incl.astype(BF16)
    onorm = onorm_ref[...]

    heads = [(n, h) for n in range(n_seq) for h in range(B_HEADS)]

    def chunk_step(ci, carry):
        r0 = pl.multiple_of(ci * CHUNK, CHUNK)
        rows = pl.ds(r0, CHUNK)
        head_cols = lambda ref: jnp.stack([ref[n, rows, h * B_DK:(h + 1) * B_DK] for n, h in heads])
        q = head_cols(gq_ref)
        k = head_cols(gk_ref)
        v = head_cols(gv_ref)
        bgc = [bg_ref[n, rows, :] for n in range(n_seq)]
        gcum = [_dot_exact_lhs(ltri, b) for b in bgc]
        lane_bcast = lambda x, c: jnp.broadcast_to(x[:, c:c + 1], (CHUNK, B_DK))
        bh = jnp.stack([lane_bcast(bgc[n], h) for n, h in heads])
        gcs = [lane_bcast(gcum[n], B_HEADS + h) for n, h in heads]
        gc = jnp.stack(gcs)
        d = gc[:, :, :CHUNK] - jnp.stack([g.T[:CHUNK, :] for g in gcs])
        decay = jnp.exp(jnp.where(incl[None], d, -jnp.inf))
        glast = gc[:, CHUNK - 1:CHUNK, :]
        eg = jnp.exp(gc)
        kb = k * bh
        kbf = k.astype(BF16)
        m = -jnp.where(strict[None], _bmm_nt(kb.astype(BF16), kbf) * decay, 0.0)
        sol = jnp.concatenate([v * bh, kb * eg], axis=2)
        sol = sol + _bmm(m.astype(BF16), sol.astype(BF16))
        p = m
        for _ in range(5):
            pb = p.astype(BF16)
            p = _bmm(pb, pb)
            sol = sol + _bmm(p.astype(BF16), sol.astype(BF16))
        u = sol[:, :, :B_DV]
        w = sol[:, :, B_DV:]
        qk = jnp.where(incl[None], _bmm_nt(q.astype(BF16), kbf) * decay, 0.0)
        s = s_ref[...]
        sb = s.astype(BF16)
        vn = (u - _bmm(w.astype(BF16), sb)).astype(BF16)
        o = _bmm((q * eg).astype(BF16), sb) + _bmm(qk.astype(BF16), vn)
        kg = (k * jnp.exp(glast - gc)).astype(BF16)
        s_ref[...] = s * jnp.exp(glast) + _bmm_tn(kg, vn)
        on = _rms_rows(o, onorm)
        for i, (n, h) in enumerate(heads):
            o_ref[n, rows, h * B_DV:(h + 1) * B_DV] = on[i]
        return carry

    lax.fori_loop(0, nc, chunk_step, 0)

    @pl.when(tb == pl.num_programs(0) - 1)
    def _():
        s_out_ref[...] = s_ref[...]


def _gdn_chunked(gq, gk, gv, bg, onorm, n_seq, seq):
    tt = min(GDN_ROWS, seq)
    assert seq % tt == 0 and tt % CHUNK == 0
    blk = lambda width: pl.BlockSpec((n_seq, tt, width), lambda t: (0, t, 0))
    state_shape = (n_seq * B_HEADS, B_DK, B_DV)
    out_shape = (jax.ShapeDtypeStruct((n_seq, seq, B_WIDTH), F32),
                 jax.ShapeDtypeStruct(state_shape, F32))
    o, s = pl.pallas_call(
        _gdn_chunk_kernel,
        grid=(seq // tt,),
        in_specs=[blk(B_QK), blk(B_QK), blk(B_WIDTH), blk(LANES), _full(onorm.shape)],
        out_specs=(blk(B_WIDTH), _full(state_shape)),
        out_shape=out_shape,
        scratch_shapes=[pltpu.VMEM(state_shape, F32)],
        compiler_params=pltpu.CompilerParams(dimension_semantics=("arbitrary",),
                                             vmem_limit_bytes=VMEM_LIMIT_BYTES),
        name="gdn_chunked",
    )(gq.reshape(n_seq, seq, B_QK), gk.reshape(n_seq, seq, B_QK), gv.reshape(n_seq, seq, B_WIDTH),
      bg.reshape(n_seq, seq, LANES), onorm)
    return o.reshape(n_seq * seq, B_WIDTH), s.reshape(n_seq, B_HEADS, B_DK, B_DV)


def _gdn_step_kernel(gq_ref, gk_ref, gv_ref, bg_ref, onorm_ref, s_ref, o_ref, so_ref):
    ri = lax.broadcasted_iota(jnp.int32, (B_DK, B_DK), 0)
    ci = lax.broadcasted_iota(jnp.int32, (B_DK, B_DK), 1)
    eye = (ri == ci).astype(F32)
    onorm = onorm_ref[...]
    for b in range(gq_ref.shape[0]):
        bg = bg_ref[b]
        for h in range(B_HEADS):
            sl = slice(h * B_DK, (h + 1) * B_DK)
            q = gq_ref[b][:, sl]
            k = gk_ref[b][:, sl]
            v = gv_ref[b][:, sl]
            beta = bg[:, h:h + 1]
            a = jnp.exp(bg[:, B_HEADS + h:B_HEADS + h + 1])
            kcol = jnp.sum(eye * k, axis=1, keepdims=True)
            qcol = jnp.sum(eye * q, axis=1, keepdims=True)
            s = s_ref[b, h] * a
            u = beta * (v - jnp.sum(s * kcol, axis=0, keepdims=True))
            s = s + kcol * u
            so_ref[b, h] = s
            o = jnp.sum(s * qcol, axis=0, keepdims=True)
            o_ref[b, :, sl] = _rms_rows(o, onorm)


def _gdn_step(gq, gk, gv, bg, onorm, state):
    n_dec = state.shape[0]
    bb = max(p for p in range(1, GDN_STEP_SEQS + 1) if n_dec % p == 0)
    row_spec = lambda width: pl.BlockSpec((bb, 1, width), lambda i: (i, 0, 0))
    s_spec = pl.BlockSpec((bb, B_HEADS, B_DK, B_DV), lambda i: (i, 0, 0, 0))
    out_shape = (jax.ShapeDtypeStruct((n_dec, 1, B_WIDTH), F32),
                 jax.ShapeDtypeStruct(state.shape, F32))
    o, s_new = pl.pallas_call(
        _gdn_step_kernel,
        grid=(n_dec // bb,),
        in_specs=[row_spec(B_QK), row_spec(B_QK), row_spec(B_WIDTH), row_spec(LANES),
                  pl.BlockSpec(onorm.shape, lambda i: (0, 0)), s_spec],
        out_specs=(row_spec(B_WIDTH), s_spec),
        out_shape=out_shape,
        compiler_params=pltpu.CompilerParams(dimension_semantics=("arbitrary",),
                                             vmem_limit_bytes=VMEM_LIMIT_BYTES),
        name="gdn_step",
    )(gq.reshape(n_dec, 1, B_QK), gk.reshape(n_dec, 1, B_QK), gv.reshape(n_dec, 1, B_WIDTH),
      bg.reshape(n_dec, 1, LANES), onorm, state.astype(F32))
    return o.reshape(n_dec, B_WIDTH), s_new


def _merge_out_kernel(x_ref, wn_ref, wg_ref, oa_ref, ob_ref, wua_ref, wub_ref, wo_ref, y_ref):
    x = x_ref[...]
    hb = _rms_rows(x, wn_ref[...]).astype(BF16)
    az = _dot(hb, wg_ref[:, 0:A_WIDTH])
    bz = _dot(hb, wg_ref[:, A_WIDTH:A_WIDTH + B_WIDTH])
    oa = oa_ref[...] * (az * _sigmoid(az))
    ob = ob_ref[...] * (bz * _sigmoid(bz))
    d = x.shape[1]
    g0 = A_WIDTH + B_WIDTH
    ga = _dot(hb, wg_ref[:, g0:g0 + d])
    gb = _dot(hb, wg_ref[:, g0 + d:g0 + 2 * d])
    y = (_sigmoid(ga) * _dot(oa.astype(BF16), wua_ref[...])
         + _sigmoid(gb) * _dot(ob.astype(BF16), wub_ref[...]))
    y_ref[...] = x + _dot(y.astype(BF16), wo_ref[...])


def _merge_out(x2, wn, w_gate, o_a, o_b, w_up_a, w_up_b, w_out, tm):
    rows, d = x2.shape
    assert rows % tm == 0
    row_spec = lambda width: pl.BlockSpec((tm, width), lambda i: (i, 0))
    return pl.pallas_call(
        _merge_out_kernel,
        grid=(rows // tm,),
        in_specs=[row_spec(d), _full(wn.shape), _full(w_gate.shape), row_spec(A_WIDTH), row_spec(B_WIDTH),
                  _full(w_up_a.shape), _full(w_up_b.shape), _full(w_out.shape)],
        out_specs=row_spec(d),
        out_shape=jax.ShapeDtypeStruct((rows, d), F32),
        compiler_params=pltpu.CompilerParams(dimension_semantics=("arbitrary",),
                                             vmem_limit_bytes=VMEM_LIMIT_BYTES),
        name="merge_out",
    )(x2, wn, w_gate, o_a, o_b, w_up_a, w_up_b, w_out)


def _rope_tables(pos):
    half = A_DH // 2
    inv = ROPE_THETA ** (-jnp.arange(half, dtype=F32) / half)
    ang = pos.astype(F32)[:, None] * inv[None, :]
    cos = jnp.cos(ang)
    sin = jnp.sin(ang)
    return jnp.tile(cos, (1, 4)), jnp.concatenate([-sin, sin, -sin, sin], axis=1)


def _repack_w_in(w):
    d = w.shape[0]
    o = 0
    parts = {}
    for name, size in (("aq", A_WIDTH), ("ak", A_WIDTH), ("av", A_WIDTH), ("az", A_WIDTH),
                       ("bq", B_QK), ("bk", B_QK), ("bv", B_WIDTH), ("bz", B_WIDTH),
                       ("bb", B_HEADS), ("ba", B_HEADS), ("ga", d), ("gb", d)):
        parts[name] = w[:, o:o + size]
        o += size
    assert o == w.shape[1]
    pad = jnp.zeros((d, LANES - 2 * B_HEADS), w.dtype)
    w_proj = jnp.concatenate([parts[n] for n in ("aq", "ak", "av", "bq", "bk", "bv", "bb", "ba")] + [pad], axis=1)
    w_gate = jnp.concatenate([parts[n] for n in ("az", "bz", "ga", "gb")], axis=1)
    return w_proj.astype(BF16), w_gate.astype(BF16)


def _lane_row(vec, offset):
    return jnp.zeros((1, LANES), F32).at[0, offset:offset + vec.shape[0]].set(vec.astype(F32))


def kernel(x_prompt, x_sample, cache_k, cache_v, state_conv, state_ssm, page_table, w_norm, w_in, a_qn, a_kn, a_lq1, a_lk1, a_lq2, a_lk2, a_subln, conv_w, a_log, dt_bias, b_onorm, w_up_a, w_up_b, w_out):
    depth = w_norm.shape[0]
    n_seq, seq, d = x_prompt.shape
    n_dec, dec_seq, _ = x_sample.shape
    assert dec_seq == 1, "the sample path handles one new token per sequence"
    n_pages = page_table.shape[1]
    past_len = n_pages * PAGE_SIZE

    cos_p, sin_p = _rope_tables(jnp.arange(seq, dtype=jnp.int32))
    cos_s, sin_s = _rope_tables(past_len + jnp.arange(dec_seq, dtype=jnp.int32))
    sub = lax.broadcasted_iota(jnp.int32, (A_WIDTH, A_WIDTH), 0) // A_DH
    gsum = jnp.where(sub == sub.T, 1.0 / A_DH, 0.0).astype(BF16)

    hp = x_prompt.reshape(n_seq * seq, d)
    hs = x_sample.reshape(n_dec, d)
    outs = [[] for _ in range(8)]
    for l in range(depth):
        lam_init = 0.8 - 0.6 * math.exp(-0.3 * l)
        wn = w_norm[l].reshape(1, d)
        w_proj, w_gate = _repack_w_in(w_in[l])
        qn = jnp.tile(a_qn[l], A_WIDTH // A_DH).reshape(1, A_WIDTH)
        kn = jnp.tile(a_kn[l], A_WIDTH // A_DH).reshape(1, A_WIDTH)
        lamp = jnp.stack([a_lq1[l], a_lk1[l], a_lq2[l], a_lk2[l]]).astype(F32)
        subln = a_subln[l].reshape(1, A_HD)
        alog = _lane_row(a_log[l], B_HEADS)
        dtb = _lane_row(dt_bias[l], B_HEADS)
        onorm = b_onorm[l].reshape(1, B_DV)
        wua = w_up_a[l].astype(BF16)
        wub = w_up_b[l].astype(BF16)
        wo = w_out[l].astype(BF16)
        convw = conv_w[l]

        q, k, kb, v, vt, gq, gk, gv, bg, conv_p = _prompt_projection(
            hp, n_seq, seq, wn, w_proj, gsum, qn, kn, cos_p, sin_p, convw, alog, dtb)
        o_a = _prompt_attention(q, kb, vt, lamp, subln, n_seq, seq, lam_init)
        o_b, ssm_p = _gdn_chunked(gq, gk, gv, bg, onorm, n_seq, seq)
        hp = _merge_out(hp, wn, w_gate, o_a, o_b, wua, wub, wo, min(MERGE_ROWS, seq))

        sc = state_conv[l]
        qs, ks, vs, gqs, gks, gvs, bgs, us = _sample_projection(
            hs, wn, w_proj, gsum, qn, kn, cos_s, sin_s, convw, alog, dtb,
            sc[:, 0, :], sc[:, 1, :], sc[:, 2, :])
        o_as = _paged_attention(qs, ks, vs, cache_k[l], cache_v[l], page_table, lamp, subln, lam_init)
        o_bs, ssm_s = _gdn_step(gqs, gks, gvs, bgs, onorm, state_ssm[l])
        hs = _merge_out(hs, wn, w_gate, o_as, o_bs, wua, wub, wo, n_dec)

        outs[0].append(k.reshape(n_seq, seq, A_HEADS, A_HD))
        outs[1].append(v.reshape(n_seq, seq, A_HEADS, A_HD))
        outs[2].append(conv_p)
        outs[3].append(ssm_p.astype(x_prompt.dtype))
        outs[4].append(ks.reshape(n_dec, dec_seq, A_HEADS, A_HD))
        outs[5].append(vs.reshape(n_dec, dec_seq, A_HEADS, A_HD))
        outs[6].append(jnp.concatenate([sc[:, 1:, :], us[:, None, :]], axis=1))
        outs[7].append(ssm_s.astype(x_sample.dtype))

    return (hp.reshape(n_seq, seq, d), hs.reshape(n_dec, dec_seq, d)) + tuple(jnp.stack(o) for o in outs)
```

```python
import functools
import math

import jax
import jax.numpy as jnp
from jax import lax
from jax.experimental import pallas as pl
from jax.experimental.pallas import tpu as pltpu

F32 = jnp.float32
BF16 = jnp.bfloat16

A_HEADS = 4
A_DH = 64
A_HD = 2 * A_DH
A_WIDTH = A_HEADS * A_HD
B_HEADS = 4
B_DK = 128
B_DV = 128
B_QK = B_HEADS * B_DK
B_WIDTH = B_HEADS * B_DV
B_CONV = 2 * B_QK + B_WIDTH
CONV_W = 4
CHUNK = 64
PAGE_SIZE = 128
ROPE_THETA = 10000.0
EPS = 1e-6

LANES = 128
SUBLANES = 8
VMEM_LIMIT_BYTES = 56 * 1024 * 1024

PROJ_ROWS = 256
ATTN_BLOCK = 512
MERGE_ROWS = 256
GDN_ROWS = 256
MAX_PAGES_PER_STEP = 16
GDN_STEP_SEQS = 4
SOLVE_SPLIT_LEVELS = 3
DENOM_ROWS = 16
Q_SCALE = (A_DH ** -0.5) * math.log2(math.e)


def _dot(a, b):
    return jnp.dot(a, b, preferred_element_type=F32)


def _dot_nt(a, b):
    return lax.dot_general(a, b, (((1,), (1,)), ((), ())), preferred_element_type=F32)


def _bmm(a, b):
    return lax.dot_general(a, b, (((2,), (1,)), ((0,), (0,))), preferred_element_type=F32)


def _bmm_split(a, b):
    ah = a.astype(BF16)
    al = (a - ah.astype(F32)).astype(BF16)
    bh = b.astype(BF16)
    bl = (b - bh.astype(F32)).astype(BF16)
    return _bmm(ah, bh) + _bmm(ah, bl) + _bmm(al, bh)


def _bmm_nt(a, b):
    return lax.dot_general(a, b, (((2,), (2,)), ((0,), (0,))), preferred_element_type=F32)


def _bmm_tn(a, b):
    return lax.dot_general(a, b, (((1,), (1,)), ((0,), (0,))), preferred_element_type=F32)


def _split3(x):
    x1 = x.astype(BF16)
    r1 = x - x1.astype(F32)
    x2 = r1.astype(BF16)
    x3 = (r1 - x2.astype(F32)).astype(BF16)
    return x1, x2, x3


def _dot_exact_lhs(mat01, x):
    x1, x2, x3 = _split3(x)
    return _dot(mat01, x1) + _dot(mat01, x2) + _dot(mat01, x3)


def _sigmoid(x):
    return 1.0 / (1.0 + jnp.exp(-x))


def _softplus(x):
    return jnp.maximum(x, 0.0) + jnp.log1p(jnp.exp(-jnp.abs(x)))


def _rms_rows(x, gain):
    return x * lax.rsqrt(jnp.mean(x * x, axis=-1, keepdims=True) + EPS) * gain


def _project_norm_rope(hb, w_ref, col0, gsum_ref, gain, cos, sin):
    a = _dot(hb, w_ref[:, col0:col0 + A_WIDTH])
    ms = _dot((a * a).astype(BF16), gsum_ref[...])
    y = a * lax.rsqrt(ms + EPS) * gain
    lane = lax.broadcasted_iota(jnp.int32, (a.shape[0], LANES), 1)
    first_half = (lane % A_DH) < (A_DH // 2)
    heads = []
    for j in range(A_HEADS):
        t = y[:, j * A_HD:(j + 1) * A_HD]
        partner = jnp.where(first_half,
                            pltpu.roll(t, LANES - A_DH // 2, 1),
                            pltpu.roll(t, A_DH // 2, 1))
        heads.append(t * cos + partner * sin)
    return heads


def _gdn_qkv_epilogue(y, gq_ref, gk_ref, gv_ref):
    act = y * _sigmoid(y)
    for h in range(B_HEADS):
        sl = slice(h * B_DK, (h + 1) * B_DK)
        qh = act[:, h * B_DK:(h + 1) * B_DK]
        kh = act[:, B_QK + h * B_DK:B_QK + (h + 1) * B_DK]
        gq_ref[:, sl] = qh * lax.rsqrt(jnp.sum(qh * qh, axis=-1, keepdims=True) + EPS) * (B_DK ** -0.5)
        gk_ref[:, sl] = kh * lax.rsqrt(jnp.sum(kh * kh, axis=-1, keepdims=True) + EPS)
    gv_ref[...] = act[:, 2 * B_QK:]


def _beta_g(z, alog_row, dtb_row):
    lane = lax.broadcasted_iota(jnp.int32, z.shape, 1)
    beta = _sigmoid(z)
    g = -jnp.exp(alog_row) * _softplus(z + dtb_row)
    return jnp.where(lane < B_HEADS, beta, jnp.where(lane < 2 * B_HEADS, g, 0.0))


def _prompt_proj_kernel(x_ref, wn_ref, w_ref, gsum_ref, qn_ref, kn_ref, cos_ref, sin_ref,
                        convw_ref, alog_ref, dtb_ref,
                        q_ref, k_ref, kb_ref, v_ref, vt_ref, gq_ref, gk_ref, gv_ref, bg_ref, conv_ref,
                        ext_ref, *, tiles_per_seq):
    i = pl.program_id(0)
    tm = x_ref.shape[0]
    hb = _rms_rows(x_ref[...], wn_ref[...]).astype(BF16)
    cos = cos_ref[...]
    sin = sin_ref[...]
    q_heads = _project_norm_rope(hb, w_ref, 0, gsum_ref, qn_ref[...], cos, sin)
    k_heads = _project_norm_rope(hb, w_ref, A_WIDTH, gsum_ref, kn_ref[...], cos, sin)
    v = _dot(hb, w_ref[:, 2 * A_WIDTH:3 * A_WIDTH])
    for h in range(A_HEADS):
        sl = slice(h * A_HD, (h + 1) * A_HD)
        q_ref[:, sl] = (q_heads[h] * Q_SCALE).astype(BF16)
        k_ref[pl.ds(h, tm, stride=A_HEADS), :] = k_heads[h]
        kb_ref[:, sl] = k_heads[h].astype(BF16)
        v_ref[pl.ds(h, tm, stride=A_HEADS), :] = v[:, sl]
        vt_ref[0, 0, h] = v[:, sl].T.astype(BF16)

    u = _dot(hb, w_ref[:, 3 * A_WIDTH:3 * A_WIDTH + B_CONV])

    @pl.when(i % tiles_per_seq == 0)
    def _():
        ext_ref[0:SUBLANES, :] = jnp.zeros((SUBLANES, B_CONV), F32)

    ext_ref[SUBLANES:SUBLANES + tm, :] = u
    w = convw_ref[...]
    y = (ext_ref[SUBLANES - 3:SUBLANES - 3 + tm, :] * w[0:1]
         + ext_ref[SUBLANES - 2:SUBLANES - 2 + tm, :] * w[1:2]
         + ext_ref[SUBLANES - 1:SUBLANES - 1 + tm, :] * w[2:3]
         + u * w[3:4])
    conv_ref[0] = ext_ref[tm + SUBLANES - (CONV_W - 1):tm + SUBLANES, :]
    ext_ref[0:SUBLANES, :] = ext_ref[tm:tm + SUBLANES, :]
    _gdn_qkv_epilogue(y, gq_ref, gk_ref, gv_ref)
    bg_ref[...] = _beta_g(_dot(hb, w_ref[:, 3 * A_WIDTH + B_CONV:]), alog_ref[...], dtb_ref[...])


def _sample_proj_kernel(x_ref, wn_ref, w_ref, gsum_ref, qn_ref, kn_ref, cos_ref, sin_ref,
                        convw_ref, alog_ref, dtb_ref, c0_ref, c1_ref, c2_ref,
                        q_ref, k_ref, v_ref, gq_ref, gk_ref, gv_ref, bg_ref, u_ref):
    hb = _rms_rows(x_ref[...], wn_ref[...]).astype(BF16)
    cos = cos_ref[...]
    sin = sin_ref[...]
    q_heads = _project_norm_rope(hb, w_ref, 0, gsum_ref, qn_ref[...], cos, sin)
    k_heads = _project_norm_rope(hb, w_ref, A_WIDTH, gsum_ref, kn_ref[...], cos, sin)
    for h in range(A_HEADS):
        sl = slice(h * A_HD, (h + 1) * A_HD)
        q_ref[:, sl] = q_heads[h] * Q_SCALE
        k_ref[:, sl] = k_heads[h]
    v_ref[...] = _dot(hb, w_ref[:, 2 * A_WIDTH:3 * A_WIDTH])
    u = _dot(hb, w_ref[:, 3 * A_WIDTH:3 * A_WIDTH + B_CONV])
    u_ref[...] = u
    w = convw_ref[...]
    y = c0_ref[...] * w[0:1] + c1_ref[...] * w[1:2] + c2_ref[...] * w[2:3] + u * w[3:4]
    _gdn_qkv_epilogue(y, gq_ref, gk_ref, gv_ref)
    bg_ref[...] = _beta_g(_dot(hb, w_ref[:, 3 * A_WIDTH + B_CONV:]), alog_ref[...], dtb_ref[...])


def _full(shape):
    return pl.BlockSpec(shape, lambda *_: (0,) * len(shape))


def _prompt_projection(x2, n_seq, seq, wn, w_proj, gsum, qn, kn, cos, sin, convw, alog, dtb):
    rows = x2.shape[0]
    tm = min(PROJ_ROWS, seq)
    assert seq % tm == 0 and tm % LANES == 0
    tps = seq // tm
    d = x2.shape[1]
    row_spec = lambda width: pl.BlockSpec((tm, width), lambda i: (i, 0))
    head_row_spec = pl.BlockSpec((A_HEADS * tm, A_HD), lambda i: (i, 0))
    pos_spec = pl.BlockSpec((tm, LANES), lambda i: (i % tps, 0))
    out_shape = (
        jax.ShapeDtypeStruct((rows, A_WIDTH), BF16),
        jax.ShapeDtypeStruct((rows * A_HEADS, A_HD), F32),
        jax.ShapeDtypeStruct((rows, A_WIDTH), BF16),
        jax.ShapeDtypeStruct((rows * A_HEADS, A_HD), F32),
        jax.ShapeDtypeStruct((n_seq, tps, A_HEADS, A_HD, tm), BF16),
        jax.ShapeDtypeStruct((rows, B_QK), F32),
        jax.ShapeDtypeStruct((rows, B_QK), F32),
        jax.ShapeDtypeStruct((rows, B_WIDTH), F32),
        jax.ShapeDtypeStruct((rows, LANES), F32),
        jax.ShapeDtypeStruct((n_seq, CONV_W - 1, B_CONV), F32),
    )
    return pl.pallas_call(
        functools.partial(_prompt_proj_kernel, tiles_per_seq=tps),
        grid=(rows // tm,),
        in_specs=[row_spec(d), _full(wn.shape), _full(w_proj.shape), _full(gsum.shape),
                  _full(qn.shape), _full(kn.shape), pos_spec, pos_spec,
                  _full(convw.shape), _full(alog.shape), _full(dtb.shape)],
        out_specs=(row_spec(A_WIDTH), head_row_spec, row_spec(A_WIDTH), head_row_spec,
                   pl.BlockSpec((1, 1, A_HEADS, A_HD, tm), lambda i: (i // tps, i % tps, 0, 0, 0)),
                   row_spec(B_QK), row_spec(B_QK), row_spec(B_WIDTH), row_spec(LANES),
                   pl.BlockSpec((1, CONV_W - 1, B_CONV), lambda i: (i // tps, 0, 0))),
        out_shape=out_shape,
        scratch_shapes=[pltpu.VMEM((tm + SUBLANES, B_CONV), F32)],
        compiler_params=pltpu.CompilerParams(dimension_semantics=("arbitrary",),
                                             vmem_limit_bytes=VMEM_LIMIT_BYTES),
        name="prompt_projection",
    )(x2, wn, w_proj, gsum, qn, kn, cos, sin, convw, alog, dtb)


def _sample_projection(x2, wn, w_proj, gsum, qn, kn, cos, sin, convw, alog, dtb, c0, c1, c2):
    rows = x2.shape[0]
    out_shape = (
        jax.ShapeDtypeStruct((rows, A_WIDTH), F32),
        jax.ShapeDtypeStruct((rows, A_WIDTH), F32),
        jax.ShapeDtypeStruct((rows, A_WIDTH), F32),
        jax.ShapeDtypeStruct((rows, B_QK), F32),
        jax.ShapeDtypeStruct((rows, B_QK), F32),
        jax.ShapeDtypeStruct((rows, B_WIDTH), F32),
        jax.ShapeDtypeStruct((rows, LANES), F32),
        jax.ShapeDtypeStruct((rows, B_CONV), F32),
    )
    args = (x2, wn, w_proj, gsum, qn, kn, cos, sin, convw, alog, dtb, c0, c1, c2)
    return pl.pallas_call(
        _sample_proj_kernel,
        grid=(1,),
        in_specs=[_full(a.shape) for a in args],
        out_specs=tuple(_full(s.shape) for s in out_shape),
        out_shape=out_shape,
        compiler_params=pltpu.CompilerParams(dimension_semantics=("arbitrary",),
                                             vmem_limit_bytes=VMEM_LIMIT_BYTES),
        name="sample_projection",
    )(*args)


def _lambda_value(lamp, lam_init):
    s1 = jnp.sum(lamp[0:1, :] * lamp[1:2, :], axis=1, keepdims=True)
    s2 = jnp.sum(lamp[2:3, :] * lamp[3:4, :], axis=1, keepdims=True)
    return jnp.exp(s1) - jnp.exp(s2) + lam_init


def _prompt_attn_kernel(lamp_ref, subln_ref, q_ref, k_ref, vt_ref, o_ref, *, lam_init):
    qb = pl.program_id(2)
    tq = q_ref.shape[1]
    tv = vt_ref.shape[-1]
    q = q_ref[0]
    lane = lax.broadcasted_iota(jnp.int32, q.shape, 1)
    zero = jnp.zeros_like(q)
    qq = jnp.concatenate([jnp.where(lane < A_DH, q, zero), jnp.where(lane >= A_DH, q, zero)], axis=0)

    def step(kv, carry, masked):
        m, l, acc = carry
        start = pl.multiple_of(kv * tq, tq)
        s = _dot_nt(k_ref[0, pl.ds(start, tq), :], qq)
        if masked:
            key = lax.broadcasted_iota(jnp.int32, s.shape, 0)
            qpos = lax.broadcasted_iota(jnp.int32, s.shape, 1) % tq
            s = jnp.where(key <= qpos, s, -jnp.inf)
        m_new = jnp.maximum(m, jnp.max(s, axis=0, keepdims=True))
        alpha = jnp.exp(m - m_new)
        p = jnp.exp(s - m_new)
        l = alpha * l + jnp.sum(p, axis=0, keepdims=True)
        pb = p.astype(BF16)
        acc = alpha * acc
        for j in range(tq // tv):
            acc = acc + _dot(vt_ref[0, kv * (tq // tv) + j, 0], pb[j * tv:(j + 1) * tv])
        return m_new, l, acc

    init = (jnp.full((1, 2 * tq), -jnp.inf, F32), jnp.zeros((1, 2 * tq), F32),
            jnp.zeros((A_HD, 2 * tq), F32))
    carry = lax.fori_loop(0, qb, lambda kv, c: step(kv, c, False), init)
    _, l, acc = step(qb, carry, True)
    o = acc / l
    lam = _lambda_value(lamp_ref[...], lam_init)
    out = (o[:, :tq] - lam * o[:, tq:]).T
    o_ref[0] = _rms_rows(out, subln_ref[...]) * (1.0 - lam_init)


def _prompt_attention(q, kb, vt, lamp, subln, n_seq, seq, lam_init):
    tv = vt.shape[-1]
    tq = min(ATTN_BLOCK, seq)
    assert seq % tq == 0 and tq % tv == 0
    q3 = q.reshape(n_seq, seq, A_WIDTH)
    k3 = kb.reshape(n_seq, seq, A_WIDTH)
    blk_spec = pl.BlockSpec((1, tq, A_HD), lambda n, h, i: (n, i, h))
    out = pl.pallas_call(
        functools.partial(_prompt_attn_kernel, lam_init=lam_init),
        grid=(n_seq, A_HEADS, seq // tq),
        in_specs=[_full(lamp.shape), _full(subln.shape), blk_spec,
                  pl.BlockSpec((1, seq, A_HD), lambda n, h, i: (n, 0, h)),
                  pl.BlockSpec((1, seq // tv, 1, A_HD, tv), lambda n, h, i: (n, 0, h, 0, 0))],
        out_specs=blk_spec,
        out_shape=jax.ShapeDtypeStruct((n_seq, seq, A_WIDTH), F32),
        compiler_params=pltpu.CompilerParams(
            dimension_semantics=("arbitrary", "arbitrary", "arbitrary"),
            vmem_limit_bytes=VMEM_LIMIT_BYTES),
        name="prompt_attention",
    )(lamp, subln, q3, k3, vt)
    return out.reshape(n_seq * seq, A_WIDTH)


def _paged_attn_kernel(pt_ref, lamp_ref, subln_ref, q_ref, kn_ref, vn_ref, *rest,
                       pages_per_step, lam_init):
    del pt_ref
    k_refs = rest[:pages_per_step]
    v_refs = rest[pages_per_step:2 * pages_per_step]
    o_ref, m_ref, l_ref, acc_ref = rest[2 * pages_per_step:]
    step = pl.program_id(1)
    rows = 2 * A_HEADS
    page_rows = PAGE_SIZE * A_HEADS
    qrow = lax.broadcasted_iota(jnp.int32, (rows, A_HD), 0)
    qlane = lax.broadcasted_iota(jnp.int32, (rows, A_HD), 1)
    qmat = jnp.where(qlane // A_DH == qrow // A_HEADS, q_ref[0], 0.0)

    @pl.when(step == 0)
    def _():
        m_ref[...] = jnp.sum(qmat * kn_ref[0], axis=1, keepdims=True)
        l_ref[...] = jnp.ones_like(l_ref)
        acc_ref[...] = vn_ref[0]

    qb = qmat.astype(BF16)
    s = jnp.concatenate([_dot_nt(qb, k_refs[j][0].astype(BF16)) for j in range(pages_per_step)], axis=1)
    row = lax.broadcasted_iota(jnp.int32, s.shape, 0)
    col = lax.broadcasted_iota(jnp.int32, s.shape, 1)
    s = jnp.where(col % A_HEADS == row % A_HEADS, s, -jnp.inf)
    m_prev = m_ref[...]
    m_new = jnp.maximum(m_prev, jnp.max(s, axis=1, keepdims=True))
    alpha = jnp.exp(m_prev - m_new)
    p = jnp.exp(s - m_new)
    l_ref[...] = alpha * l_ref[...] + jnp.sum(p, axis=1, keepdims=True)
    pb = p.astype(BF16)
    pv = _dot(pb[:, 0:page_rows], v_refs[0][0].astype(BF16))
    for j in range(1, pages_per_step):
        pv = pv + _dot(pb[:, j * page_rows:(j + 1) * page_rows], v_refs[j][0].astype(BF16))
    acc_ref[...] = alpha * acc_ref[...] + pv
    m_ref[...] = m_new

    @pl.when(step == pl.num_programs(1) - 1)
    def _():
        o = acc_ref[...] / l_ref[...]
        lam = _lambda_value(lamp_ref[...], lam_init)
        out = o[:A_HEADS] - lam * o[A_HEADS:]
        o_ref[0] = _rms_rows(out, subln_ref[...]) * (1.0 - lam_init)


def _paged_attention(q, k_new, v_new, cache_k, cache_v, page_table, lamp, subln, lam_init):
    n_dec, n_pages = page_table.shape
    n_pool = cache_k.shape[0]
    pps = max(p for p in range(1, MAX_PAGES_PER_STEP + 1) if n_pages % p == 0)
    ck = cache_k.reshape(n_pool, PAGE_SIZE * A_HEADS, A_HD)
    cv = cache_v.reshape(n_pool, PAGE_SIZE * A_HEADS, A_HD)
    stack2 = lambda a: jnp.tile(a.reshape(n_dec, A_HEADS, A_HD), (1, 2, 1))
    rows = 2 * A_HEADS
    row_spec = pl.BlockSpec((1, rows, A_HD), lambda b, s, pt: (b, 0, 0))

    def page_spec(j):
        return pl.BlockSpec((1, PAGE_SIZE * A_HEADS, A_HD),
                            lambda b, s, pt: (pt[b * n_pages + s * pps + j], 0, 0))

    const = lambda shape: pl.BlockSpec(shape, lambda b, s, pt: (0,) * len(shape))
    grid_spec = pltpu.PrefetchScalarGridSpec(
        num_scalar_prefetch=1,
        grid=(n_dec, n_pages // pps),
        in_specs=[const(lamp.shape), const(subln.shape), row_spec, row_spec, row_spec]
                 + [page_spec(j) for j in range(pps)] + [page_spec(j) for j in range(pps)],
        out_specs=pl.BlockSpec((1, A_HEADS, A_HD), lambda b, s, pt: (b, 0, 0)),
        scratch_shapes=[pltpu.VMEM((rows, 1), F32), pltpu.VMEM((rows, 1), F32),
                        pltpu.VMEM((rows, A_HD), F32)],
    )
    out = pl.pallas_call(
        functools.partial(_paged_attn_kernel, pages_per_step=pps, lam_init=lam_init),
        grid_spec=grid_spec,
        out_shape=jax.ShapeDtypeStruct((n_dec, A_HEADS, A_HD), F32),
        compiler_params=pltpu.CompilerParams(dimension_semantics=("arbitrary", "arbitrary"),
                                             vmem_limit_bytes=VMEM_LIMIT_BYTES),
        name="paged_attention",
    )(page_table.reshape(-1), lamp, subln, stack2(q), stack2(k_new), stack2(v_new),
      *([ck] * pps), *([cv] * pps))
    return out.reshape(n_dec, A_WIDTH)


def _flash_update(s, m_ref, l_ref, acc_ref, vt_ref):
    m = m_ref[...]
    m_new = jnp.maximum(m, jnp.max(s, axis=0, keepdims=True))
    alpha = jnp.exp(m - m_new)
    p = jnp.exp(s - m_new)
    l_ref[...] = alpha * l_ref[...] + jnp.sum(p, axis=0, keepdims=True)
    pb = p.astype(BF16)
    tv = vt_ref.shape[-1]
    acc = alpha * acc_ref[...]
    for j in range(vt_ref.shape[1]):
        acc = acc + _dot(vt_ref[0, j, 0], pb[j * tv:(j + 1) * tv])
    acc_ref[...] = acc
    m_ref[...] = m_new


def _dual_attn_kernel(pt_ref, tn_ref, th_ref, tqb_ref, tkv_ref,
                      lamp_ref, subln_ref, q_ref, k_ref, vt_ref, q8_ref, k8_ref, v8_ref, *rest,
                      pages_per_step, paged_steps, chunks_per_seq, lam_init):
    del tn_ref, th_ref
    (ck_ref, cv_ref, o_ref, os_ref, m_ref, acc_ref, sm_ref, sl_ref, sacc_ref,
     kbuf_ref, vbuf_ref, ksem_ref, vsem_ref) = rest
    step = pl.program_id(0)
    qb = tqb_ref[step]
    kv = tkv_ref[step]
    tq = q_ref.shape[1]
    lam = _lambda_value(lamp_ref[...], lam_init)
    subln = subln_ref[...]
    active = step < paged_steps
    slot = lax.rem(step, 2)

    def page_copies(page_step, buf):
        copies = []
        for j in range(pages_per_step):
            page = pt_ref[page_step * pages_per_step + j]
            copies.append(pltpu.make_async_copy(ck_ref.at[page], kbuf_ref.at[buf, j], ksem_ref.at[buf]))
            copies.append(pltpu.make_async_copy(cv_ref.at[page], vbuf_ref.at[buf, j], vsem_ref.at[buf]))
        return copies

    @pl.when(step == 0)
    def _():
        for c in page_copies(0, 0):
            c.start()

    @pl.when(step + 1 < paged_steps)
    def _():
        for c in page_copies(step + 1, 1 - slot):
            c.start()

    @pl.when(active)
    def _():
        for c in page_copies(step, slot):
            c.wait()

    @pl.when(kv == 0)
    def _():
        m_ref[...] = jnp.full(m_ref.shape, -jnp.inf, F32)
        acc_ref[...] = jnp.zeros_like(acc_ref)

    chunk = lax.rem(step, chunks_per_seq)
    rows = 2 * A_HEADS
    page_rows = PAGE_SIZE * A_HEADS
    qrow = lax.broadcasted_iota(jnp.int32, (rows, A_HD), 0)
    qlane = lax.broadcasted_iota(jnp.int32, (rows, A_HD), 1)
    qmat = jnp.where(qlane // A_DH == qrow // A_HEADS, q8_ref[0], 0.0)

    @pl.when(jnp.logical_and(active, chunk == 0))
    def _():
        sm_ref[...] = jnp.sum(qmat * k8_ref[0], axis=1, keepdims=True)
        sl_ref[...] = jnp.ones_like(sl_ref)
        sacc_ref[...] = v8_ref[0]

    def dual_step(masked):
        q = q_ref[0]
        lane = lax.broadcasted_iota(jnp.int32, q.shape, 1)
        zero = jnp.zeros_like(q)
        kblk = k_ref[0]
        s_half = [_dot_nt(kblk, jnp.where(lane < A_DH, q, zero)),
                  _dot_nt(kblk, jnp.where(lane >= A_DH, q, zero))]
        qmb = qmat.astype(BF16)
        sc = jnp.concatenate([_dot_nt(qmb, kbuf_ref[slot, j].astype(BF16))
                              for j in range(pages_per_step)], axis=1)
        if masked:
            causal = (lax.broadcasted_iota(jnp.int32, (tq, tq), 0)
                      <= lax.broadcasted_iota(jnp.int32, (tq, tq), 1))
        tv = vt_ref.shape[-1]
        ones_rows = jnp.ones((acc_ref.shape[0] - A_HD, tv), BF16)
        for half in range(2):
            cols = slice(half * tq, (half + 1) * tq)
            s = jnp.where(causal, s_half[half], -jnp.inf) if masked else s_half[half]
            m_prev = m_ref[:, cols]
            m_new = jnp.maximum(m_prev, jnp.max(s, axis=0, keepdims=True))
            alpha = jnp.exp2(m_prev - m_new)
            pb = jnp.exp2(s - m_new).astype(BF16)
            m_ref[:, cols] = m_new
            acc = alpha * acc_ref[:, cols]
            for j in range(vt_ref.shape[1]):
                vt_ext = jnp.concatenate([vt_ref[0, j, 0], ones_rows], axis=0)
                acc = acc + _dot(vt_ext, pb[j * tv:(j + 1) * tv])
            acc_ref[:, cols] = acc
        row = lax.broadcasted_iota(jnp.int32, sc.shape, 0)
        col = lax.broadcasted_iota(jnp.int32, sc.shape, 1)
        sc = jnp.where(col % A_HEADS == row % A_HEADS, sc, -jnp.inf)
        sm_prev = sm_ref[...]
        sl_prev = sl_ref[...]
        sacc_prev = sacc_ref[...]
        sm_new = jnp.maximum(sm_prev, jnp.max(sc, axis=1, keepdims=True))
        salpha = jnp.exp2(sm_prev - sm_new)
        sp = jnp.exp2(sc - sm_new)
        spb = sp.astype(BF16)
        pv = _dot(spb[:, 0:page_rows], vbuf_ref[slot, 0].astype(BF16))
        for j in range(1, pages_per_step):
            pv = pv + _dot(spb[:, j * page_rows:(j + 1) * page_rows], vbuf_ref[slot, j].astype(BF16))
        sl_ref[...] = jnp.where(active, salpha * sl_prev + jnp.sum(sp, axis=1, keepdims=True), sl_prev)
        sacc_ref[...] = jnp.where(active, salpha * sacc_prev + pv, sacc_prev)
        sm_ref[...] = jnp.where(active, sm_new, sm_prev)

    @pl.when(kv < qb)
    def _():
        dual_step(False)

    @pl.when(kv == qb)
    def _():
        dual_step(True)
        o = acc_ref[0:A_HD, :] / acc_ref[A_HD:A_HD + 1, :]
        out = (o[:, :tq] - lam * o[:, tq:]).T
        o_ref[0] = _rms_rows(out, subln) * (1.0 - lam_init)

    @pl.when(jnp.logical_and(active, chunk == chunks_per_seq - 1))
    def _():
        o = sacc_ref[...] / sl_ref[...]
        out = o[:A_HEADS] - lam * o[A_HEADS:]
        os_ref[0] = _rms_rows(out, subln) * (1.0 - lam_init)


def _dual_attention(q, kb, vt, q_s, k_s, v_s, cache_k, cache_v, page_table, lamp, subln,
                    n_seq, seq, lam_init):
    tv = vt.shape[-1]
    tq = min(ATTN_BLOCK, seq)
    assert seq % tq == 0 and tq % tv == 0
    nq = seq // tq
    n_dec, n_pages = page_table.shape
    n_pool = cache_k.shape[0]
    pairs = [(n, h, i, j) for n in range(n_seq) for h in range(A_HEADS)
             for i in range(nq) for j in range(i + 1)]
    steps = len(pairs)
    fits = [p for p in range(1, n_pages + 1) if n_pages % p == 0 and n_dec * (n_pages // p) <= steps]
    assert fits, "prompt attention grid too short to carry the cache page stream"
    pps = fits[0]
    cps = n_pages // pps
    paged_steps = n_dec * cps
    tabs = [jnp.asarray([p[c] for p in pairs], jnp.int32) for c in range(4)]

    q3 = q.reshape(n_seq, seq, A_WIDTH)
    k3 = kb.reshape(n_seq, seq, A_WIDTH)
    ck = cache_k.reshape(n_pool, PAGE_SIZE * A_HEADS, A_HD)
    cv = cache_v.reshape(n_pool, PAGE_SIZE * A_HEADS, A_HD)
    stack2 = lambda a: jnp.tile(a.reshape(n_dec, A_HEADS, A_HD), (1, 2, 1))
    rows = 2 * A_HEADS

    pstep = lambda s: jnp.minimum(s, paged_steps - 1)
    const = lambda shape: pl.BlockSpec(shape, lambda s, *_: (0,) * len(shape))
    q_spec = pl.BlockSpec((1, tq, A_HD), lambda s, pt, tn, th, tqb, tkv: (tn[s], tqb[s], th[s]))
    k_spec = pl.BlockSpec((1, tq, A_HD), lambda s, pt, tn, th, tqb, tkv: (tn[s], tkv[s], th[s]))
    vt_spec = pl.BlockSpec((1, tq // tv, 1, A_HD, tv),
                           lambda s, pt, tn, th, tqb, tkv: (tn[s], tkv[s], th[s], 0, 0))
    row_spec = pl.BlockSpec((1, rows, A_HD), lambda s, *_: (pstep(s) // cps, 0, 0))
    hbm_spec = pl.BlockSpec(memory_space=pl.ANY)
    page_buf = pltpu.VMEM((2, pps, PAGE_SIZE * A_HEADS, A_HD), F32)

    grid_spec = pltpu.PrefetchScalarGridSpec(
        num_scalar_prefetch=5,
        grid=(steps,),
        in_specs=[const(lamp.shape), const(subln.shape), q_spec, k_spec, vt_spec,
                  row_spec, row_spec, row_spec, hbm_spec, hbm_spec],
        out_specs=(q_spec, pl.BlockSpec((1, A_HEADS, A_HD), lambda s, *_: (pstep(s) // cps, 0, 0))),
        scratch_shapes=[pltpu.VMEM((1, 2 * tq), F32),
                        pltpu.VMEM((A_HD + DENOM_ROWS, 2 * tq), F32),
                        pltpu.VMEM((rows, 1), F32), pltpu.VMEM((rows, 1), F32),
                        pltpu.VMEM((rows, A_HD), F32),
                        page_buf, page_buf,
                        pltpu.SemaphoreType.DMA((2,)), pltpu.SemaphoreType.DMA((2,))],
    )
    o_p, o_s = pl.pallas_call(
        functools.partial(_dual_attn_kernel, pages_per_step=pps, paged_steps=paged_steps,
                          chunks_per_seq=cps, lam_init=lam_init),
        grid_spec=grid_spec,
        out_shape=(jax.ShapeDtypeStruct((n_seq, seq, A_WIDTH), F32),
                   jax.ShapeDtypeStruct((n_dec, A_HEADS, A_HD), F32)),
        compiler_params=pltpu.CompilerParams(dimension_semantics=("arbitrary",),
                                             vmem_limit_bytes=VMEM_LIMIT_BYTES),
        name="dual_attention",
    )(page_table.reshape(-1), *tabs, lamp, subln, q3, k3, vt, stack2(q_s), stack2(k_s), stack2(v_s), ck, cv)
    return o_p.reshape(n_seq * seq, A_WIDTH), o_s.reshape(n_dec, A_WIDTH)


def _gdn_chunk_kernel(gq_ref, gk_ref, gv_ref, bg_ref, onorm_ref, o_ref, s_out_ref, s_ref):
    tb = pl.program_id(0)
    n_seq, tt, _ = gq_ref.shape
    nc = tt // CHUNK

    @pl.when(tb == 0)
    def _():
        s_ref[...] = jnp.zeros_like(s_ref)

    ii = lax.broadcasted_iota(jnp.int32, (CHUNK, CHUNK), 0)
    jj = lax.broadcasted_iota(jnp.int32, (CHUNK, CHUNK), 1)
    incl = ii >= jj
    strict = ii > jj
    ltri = incl.astype(BF16)
    onorm = onorm_ref[...]

    heads = [(n, h) for n in range(n_seq) for h in range(B_HEADS)]

    def chunk_step(ci, carry):
        r0 = pl.multiple_of(ci * CHUNK, CHUNK)
        rows = pl.ds(r0, CHUNK)
        head_cols = lambda ref: jnp.stack([ref[n, rows, h * B_DK:(h + 1) * B_DK] for n, h in heads])
        q = head_cols(gq_ref)
        k = head_cols(gk_ref)
        v = head_cols(gv_ref)
        bgc = [bg_ref[n, rows, :] for n in range(n_seq)]
        gcum = [_dot_exact_lhs(ltri, b) for b in bgc]
        lane_bcast = lambda x, c: jnp.broadcast_to(x[:, c:c + 1], (CHUNK, B_DK))
        bh = jnp.stack([lane_bcast(bgc[n], h) for n, h in heads])
        gcs = [lane_bcast(gcum[n], B_HEADS + h) for n, h in heads]
        gc = jnp.stack(gcs)
        d = gc[:, :, :CHUNK] - jnp.stack([g.T[:CHUNK, :] for g in gcs])
        decay = jnp.exp(jnp.where(incl[None], d, -jnp.inf))
        glast = gc[:, CHUNK - 1:CHUNK, :]
        eg = jnp.exp(gc)
        kb = k * bh
        kbf = k.astype(BF16)
        m = -jnp.where(strict[None], _bmm_nt(kb.astype(BF16), kbf) * decay, 0.0)
        sol = jnp.concatenate([v * bh, kb * eg], axis=2)
        mm = lambda level, a, b: (_bmm_split(a, b) if level < SOLVE_SPLIT_LEVELS
                                  else _bmm(a.astype(BF16), b.astype(BF16)))
        sol = sol + mm(0, m, sol)
        p = m
        for level in range(5):
            p = mm(level, p, p)
            sol = sol + mm(level + 1, p, sol)
        u = sol[:, :, :B_DV]
        w = sol[:, :, B_DV:]
        qk = jnp.where(incl[None], _bmm_nt(q.astype(BF16), kbf) * decay, 0.0)
        s = s_ref[...]
        sb = s.astype(BF16)
        vn = (u - _bmm(w.astype(BF16), sb)).astype(BF16)
        o = _bmm((q * eg).astype(BF16), sb) + _bmm(qk.astype(BF16), vn)
        kg = (k * jnp.exp(glast - gc)).astype(BF16)
        s_ref[...] = s * jnp.exp(glast) + _bmm_tn(kg, vn)
        on = _rms_rows(o, onorm)
        for i, (n, h) in enumerate(heads):
            o_ref[n, rows, h * B_DV:(h + 1) * B_DV] = on[i]
        return carry

    lax.fori_loop(0, nc, chunk_step, 0)

    @pl.when(tb == pl.num_programs(0) - 1)
    def _():
        s_out_ref[...] = s_ref[...]


def _gdn_chunked(gq, gk, gv, bg, onorm, n_seq, seq):
    tt = min(GDN_ROWS, seq)
    assert seq % tt == 0 and tt % CHUNK == 0
    blk = lambda width: pl.BlockSpec((n_seq, tt, width), lambda t: (0, t, 0))
    state_shape = (n_seq * B_HEADS, B_DK, B_DV)
    out_shape = (jax.ShapeDtypeStruct((n_seq, seq, B_WIDTH), F32),
                 jax.ShapeDtypeStruct(state_shape, F32))
    o, s = pl.pallas_call(
        _gdn_chunk_kernel,
        grid=(seq // tt,),
        in_specs=[blk(B_QK), blk(B_QK), blk(B_WIDTH), blk(LANES), _full(onorm.shape)],
        out_specs=(blk(B_WIDTH), _full(state_shape)),
        out_shape=out_shape,
        scratch_shapes=[pltpu.VMEM(state_shape, F32)],
        compiler_params=pltpu.CompilerParams(dimension_semantics=("arbitrary",),
                                             vmem_limit_bytes=VMEM_LIMIT_BYTES),
        name="gdn_chunked",
    )(gq.reshape(n_seq, seq, B_QK), gk.reshape(n_seq, seq, B_QK), gv.reshape(n_seq, seq, B_WIDTH),
      bg.reshape(n_seq, seq, LANES), onorm)
    return o.reshape(n_seq * seq, B_WIDTH), s.reshape(n_seq, B_HEADS, B_DK, B_DV)


def _gdn_step_kernel(gq_ref, gk_ref, gv_ref, bg_ref, onorm_ref, s_ref, o_ref, so_ref):
    ri = lax.broadcasted_iota(jnp.int32, (B_DK, B_DK), 0)
    ci = lax.broadcasted_iota(jnp.int32, (B_DK, B_DK), 1)
    eye = (ri == ci).astype(F32)
    onorm = onorm_ref[...]
    for b in range(gq_ref.shape[0]):
        bg = bg_ref[b]
        for h in range(B_HEADS):
            sl = slice(h * B_DK, (h + 1) * B_DK)
            q = gq_ref[b][:, sl]
            k = gk_ref[b][:, sl]
            v = gv_ref[b][:, sl]
            beta = bg[:, h:h + 1]
            a = jnp.exp(bg[:, B_HEADS + h:B_HEADS + h + 1])
            kcol = jnp.sum(eye * k, axis=1, keepdims=True)
            qcol = jnp.sum(eye * q, axis=1, keepdims=True)
            s = s_ref[b, h] * a
            u = beta * (v - jnp.sum(s * kcol, axis=0, keepdims=True))
            s = s + kcol * u
            so_ref[b, h] = s
            o = jnp.sum(s * qcol, axis=0, keepdims=True)
            o_ref[b, :, sl] = _rms_rows(o, onorm)


def _gdn_step(gq, gk, gv, bg, onorm, state):
    n_dec = state.shape[0]
    bb = max(p for p in range(1, GDN_STEP_SEQS + 1) if n_dec % p == 0)
    row_spec = lambda width: pl.BlockSpec((bb, 1, width), lambda i: (i, 0, 0))
    s_spec = pl.BlockSpec((bb, B_HEADS, B_DK, B_DV), lambda i: (i, 0, 0, 0))
    out_shape = (jax.ShapeDtypeStruct((n_dec, 1, B_WIDTH), F32),
                 jax.ShapeDtypeStruct(state.shape, F32))
    o, s_new = pl.pallas_call(
        _gdn_step_kernel,
        grid=(n_dec // bb,),
        in_specs=[row_spec(B_QK), row_spec(B_QK), row_spec(B_WIDTH), row_spec(LANES),
                  pl.BlockSpec(onorm.shape, lambda i: (0, 0)), s_spec],
        out_specs=(row_spec(B_WIDTH), s_spec),
        out_shape=out_shape,
        compiler_params=pltpu.CompilerParams(dimension_semantics=("arbitrary",),
                                             vmem_limit_bytes=VMEM_LIMIT_BYTES),
        name="gdn_step",
    )(gq.reshape(n_dec, 1, B_QK), gk.reshape(n_dec, 1, B_QK), gv.reshape(n_dec, 1, B_WIDTH),
      bg.reshape(n_dec, 1, LANES), onorm, state.astype(F32))
    return o.reshape(n_dec, B_WIDTH), s_new


def _merge_out_kernel(x_ref, wn_ref, wg_ref, oa_ref, ob_ref, wua_ref, wub_ref, wo_ref, y_ref):
    x = x_ref[...]
    hb = _rms_rows(x, wn_ref[...]).astype(BF16)
    az = _dot(hb, wg_ref[:, 0:A_WIDTH])
    bz = _dot(hb, wg_ref[:, A_WIDTH:A_WIDTH + B_WIDTH])
    oa = oa_ref[...] * (az * _sigmoid(az))
    ob = ob_ref[...] * (bz * _sigmoid(bz))
    d = x.shape[1]
    g0 = A_WIDTH + B_WIDTH
    ga = _dot(hb, wg_ref[:, g0:g0 + d])
    gb = _dot(hb, wg_ref[:, g0 + d:g0 + 2 * d])
    y = (_sigmoid(ga) * _dot(oa.astype(BF16), wua_ref[...])
         + _sigmoid(gb) * _dot(ob.astype(BF16), wub_ref[...]))
    y_ref[...] = x + _dot(y.astype(BF16), wo_ref[...])


def _merge_out(x2, wn, w_gate, o_a, o_b, w_up_a, w_up_b, w_out, tm):
    rows, d = x2.shape
    assert rows % tm == 0
    row_spec = lambda width: pl.BlockSpec((tm, width), lambda i: (i, 0))
    return pl.pallas_call(
        _merge_out_kernel,
        grid=(rows // tm,),
        in_specs=[row_spec(d), _full(wn.shape), _full(w_gate.shape), row_spec(A_WIDTH), row_spec(B_WIDTH),
                  _full(w_up_a.shape), _full(w_up_b.shape), _full(w_out.shape)],
        out_specs=row_spec(d),
        out_shape=jax.ShapeDtypeStruct((rows, d), F32),
        compiler_params=pltpu.CompilerParams(dimension_semantics=("arbitrary",),
                                             vmem_limit_bytes=VMEM_LIMIT_BYTES),
        name="merge_out",
    )(x2, wn, w_gate, o_a, o_b, w_up_a, w_up_b, w_out)


def _rope_tables(pos):
    half = A_DH // 2
    inv = ROPE_THETA ** (-jnp.arange(half, dtype=F32) / half)
    ang = pos.astype(F32)[:, None] * inv[None, :]
    cos = jnp.cos(ang)
    sin = jnp.sin(ang)
    return jnp.tile(cos, (1, 4)), jnp.concatenate([-sin, sin, -sin, sin], axis=1)


def _repack_w_in(w):
    d = w.shape[0]
    o = 0
    parts = {}
    for name, size in (("aq", A_WIDTH), ("ak", A_WIDTH), ("av", A_WIDTH), ("az", A_WIDTH),
                       ("bq", B_QK), ("bk", B_QK), ("bv", B_WIDTH), ("bz", B_WIDTH),
                       ("bb", B_HEADS), ("ba", B_HEADS), ("ga", d), ("gb", d)):
        parts[name] = w[:, o:o + size]
        o += size
    assert o == w.shape[1]
    pad = jnp.zeros((d, LANES - 2 * B_HEADS), w.dtype)
    w_proj = jnp.concatenate([parts[n] for n in ("aq", "ak", "av", "bq", "bk", "bv", "bb", "ba")] + [pad], axis=1)
    w_gate = jnp.concatenate([parts[n] for n in ("az", "bz", "ga", "gb")], axis=1)
    return w_proj.astype(BF16), w_gate.astype(BF16)


def _lane_row(vec, offset):
    return jnp.zeros((1, LANES), F32).at[0, offset:offset + vec.shape[0]].set(vec.astype(F32))


def kernel(x_prompt, x_sample, cache_k, cache_v, state_conv, state_ssm, page_table, w_norm, w_in, a_qn, a_kn, a_lq1, a_lk1, a_lq2, a_lk2, a_subln, conv_w, a_log, dt_bias, b_onorm, w_up_a, w_up_b, w_out):
    depth = w_norm.shape[0]
    n_seq, seq, d = x_prompt.shape
    n_dec, dec_seq, _ = x_sample.shape
    assert dec_seq == 1, "the sample path handles one new token per sequence"
    n_pages = page_table.shape[1]
    past_len = n_pages * PAGE_SIZE

    cos_p, sin_p = _rope_tables(jnp.arange(seq, dtype=jnp.int32))
    cos_s, sin_s = _rope_tables(past_len + jnp.arange(dec_seq, dtype=jnp.int32))
    sub = lax.broadcasted_iota(jnp.int32, (A_WIDTH, A_WIDTH), 0) // A_DH
    gsum = jnp.where(sub == sub.T, 1.0 / A_DH, 0.0).astype(BF16)

    hp = x_prompt.reshape(n_seq * seq, d)
    hs = x_sample.reshape(n_dec, d)
    outs = [[] for _ in range(8)]
    for l in range(depth):
        lam_init = 0.8 - 0.6 * math.exp(-0.3 * l)
        wn = w_norm[l].reshape(1, d)
        w_proj, w_gate = _repack_w_in(w_in[l])
        qn = jnp.tile(a_qn[l], A_WIDTH // A_DH).reshape(1, A_WIDTH)
        kn = jnp.tile(a_kn[l], A_WIDTH // A_DH).reshape(1, A_WIDTH)
        lamp = jnp.stack([a_lq1[l], a_lk1[l], a_lq2[l], a_lk2[l]]).astype(F32)
        subln = a_subln[l].reshape(1, A_HD)
        alog = _lane_row(a_log[l], B_HEADS)
        dtb = _lane_row(dt_bias[l], B_HEADS)
        onorm = b_onorm[l].reshape(1, B_DV)
        wua = w_up_a[l].astype(BF16)
        wub = w_up_b[l].astype(BF16)
        wo = w_out[l].astype(BF16)
        convw = conv_w[l]

        q, k, kb, v, vt, gq, gk, gv, bg, conv_p = _prompt_projection(
            hp, n_seq, seq, wn, w_proj, gsum, qn, kn, cos_p, sin_p, convw, alog, dtb)
        sc = state_conv[l]
        qs, ks, vs, gqs, gks, gvs, bgs, us = _sample_projection(
            hs, wn, w_proj, gsum, qn, kn, cos_s, sin_s, convw, alog, dtb,
            sc[:, 0, :], sc[:, 1, :], sc[:, 2, :])
        o_a, o_as = _dual_attention(q, kb, vt, qs, ks, vs, cache_k[l], cache_v[l], page_table,
                                    lamp, subln, n_seq, seq, lam_init)
        o_b, ssm_p = _gdn_chunked(gq, gk, gv, bg, onorm, n_seq, seq)
        hp = _merge_out(hp, wn, w_gate, o_a, o_b, wua, wub, wo, min(MERGE_ROWS, seq))
        o_bs, ssm_s = _gdn_step(gqs, gks, gvs, bgs, onorm, state_ssm[l])
        hs = _merge_out(hs, wn, w_gate, o_as, o_bs, wua, wub, wo, n_dec)

        outs[0].append(k.reshape(n_seq, seq, A_HEADS, A_HD))
        outs[1].append(v.reshape(n_seq, seq, A_HEADS, A_HD))
        outs[2].append(conv_p)
        outs[3].append(ssm_p.astype(x_prompt.dtype))
        outs[4].append(ks.reshape(n_dec, dec_seq, A_HEADS, A_HD))
        outs[5].append(vs.reshape(n_dec, dec_seq, A_HEADS, A_HD))
        outs[6].append(jnp.concatenate([sc[:, 1:, :], us[:, None, :]], axis=1))
        outs[7].append(ssm_s.astype(x_sample.dtype))

    return (hp.reshape(n_seq, seq, d), hs.reshape(n_dec, dec_seq, d)) + tuple(jnp.stack(o) for o in outs)
```

```python
import functools
import math

import jax
import jax.numpy as jnp
from jax import lax
from jax.experimental import pallas as pl
from jax.experimental.pallas import tpu as pltpu

F32 = jnp.float32
BF16 = jnp.bfloat16

A_HEADS = 4
A_DH = 64
A_HD = 2 * A_DH
A_WIDTH = A_HEADS * A_HD
B_HEADS = 4
B_DK = 128
B_DV = 128
B_QK = B_HEADS * B_DK
B_WIDTH = B_HEADS * B_DV
B_CONV = 2 * B_QK + B_WIDTH
CONV_W = 4
CHUNK = 64
PAGE_SIZE = 128
ROPE_THETA = 10000.0
EPS = 1e-6

LANES = 128
SUBLANES = 8
VMEM_LIMIT_BYTES = 56 * 1024 * 1024

PROJ_ROWS = 256
ATTN_BLOCK = 512
MERGE_ROWS = 256
GDN_ROWS = 256
PAGE_RING = 4
GDN_STEP_SEQS = 4
SOLVE_SPLIT_LEVELS = 3
DENOM_ROWS = 16
Q_SCALE = (A_DH ** -0.5) * math.log2(math.e)


def _dot(a, b):
    return jnp.dot(a, b, preferred_element_type=F32)


def _dot_nt(a, b):
    return lax.dot_general(a, b, (((1,), (1,)), ((), ())), preferred_element_type=F32)


def _bmm(a, b):
    return lax.dot_general(a, b, (((2,), (1,)), ((0,), (0,))), preferred_element_type=F32)


def _bmm_split(a, b):
    ah = a.astype(BF16)
    al = (a - ah.astype(F32)).astype(BF16)
    bh = b.astype(BF16)
    bl = (b - bh.astype(F32)).astype(BF16)
    return _bmm(ah, bh) + _bmm(ah, bl) + _bmm(al, bh)


def _bmm_nt(a, b):
    return lax.dot_general(a, b, (((2,), (2,)), ((0,), (0,))), preferred_element_type=F32)


def _bmm_tn(a, b):
    return lax.dot_general(a, b, (((1,), (1,)), ((0,), (0,))), preferred_element_type=F32)


def _split3(x):
    x1 = x.astype(BF16)
    r1 = x - x1.astype(F32)
    x2 = r1.astype(BF16)
    x3 = (r1 - x2.astype(F32)).astype(BF16)
    return x1, x2, x3


def _dot_exact_lhs(mat01, x):
    x1, x2, x3 = _split3(x)
    return _dot(mat01, x1) + _dot(mat01, x2) + _dot(mat01, x3)


def _sigmoid(x):
    return 1.0 / (1.0 + jnp.exp(-x))


def _softplus(x):
    return jnp.maximum(x, 0.0) + jnp.log1p(jnp.exp(-jnp.abs(x)))


def _rms_rows(x, gain):
    return x * lax.rsqrt(jnp.mean(x * x, axis=-1, keepdims=True) + EPS) * gain


def _sub_head_mean_sq(a, gsum_ref):
    return _dot((a * a).astype(BF16), gsum_ref[...])


def _norm_rope_heads(a, ms, gain, cos, sin):
    y = a * lax.rsqrt(ms + EPS) * gain
    lane = lax.broadcasted_iota(jnp.int32, (a.shape[0], LANES), 1)
    first_half = (lane % A_DH) < (A_DH // 2)
    heads = []
    for j in range(A_HEADS):
        t = y[:, j * A_HD:(j + 1) * A_HD]
        partner = jnp.where(first_half,
                            pltpu.roll(t, LANES - A_DH // 2, 1),
                            pltpu.roll(t, A_DH // 2, 1))
        heads.append(t * cos + partner * sin)
    return heads


def _conv_silu(taps, w, cols):
    y = taps[0] * w[0:1, cols]
    for i in range(1, CONV_W):
        y = y + taps[i] * w[i:i + 1, cols]
    return y * _sigmoid(y)


def _store_gdn_tile(t, c, gq_ref, gk_ref, gv_ref):
    group, h = divmod(c, B_HEADS)
    if group < 2:
        t = t * lax.rsqrt(jnp.sum(t * t, axis=-1, keepdims=True) + EPS)
    if group == 0:
        t = t * (B_DK ** -0.5)
    (gq_ref, gk_ref, gv_ref)[group][:, h * B_DK:(h + 1) * B_DK] = t


def _beta_g(z, alog_row, dtb_row):
    lane = lax.broadcasted_iota(jnp.int32, z.shape, 1)
    beta = _sigmoid(z)
    g = -jnp.exp(alog_row) * _softplus(z + dtb_row)
    return jnp.where(lane < B_HEADS, beta, jnp.where(lane < 2 * B_HEADS, g, 0.0))


W_Q0, W_K0, W_V0, W_U0, W_BG0 = 0, A_WIDTH, 2 * A_WIDTH, 3 * A_WIDTH, 3 * A_WIDTH + B_CONV


def _prompt_proj_kernel(x_ref, wn_ref, w_ref, gsum_ref, qn_ref, kn_ref, cos_ref, sin_ref,
                        convw_ref, alog_ref, dtb_ref,
                        q_ref, k_ref, kb_ref, v_ref, vt_ref, gq_ref, gk_ref, gv_ref, bg_ref, conv_ref,
                        ext_ref, *, tiles_per_seq):
    i = pl.program_id(0)
    tm = x_ref.shape[0]
    hb = _rms_rows(x_ref[...], wn_ref[...]).astype(BF16)
    cos = cos_ref[...]
    sin = sin_ref[...]
    w = convw_ref[...]

    @pl.when(i % tiles_per_seq == 0)
    def _():
        ext_ref[0:SUBLANES, :] = jnp.zeros((SUBLANES, B_CONV), F32)

    def conv_tiles(chunk, u_chunk):
        for half in range(2):
            c = 2 * chunk + half
            cols = slice(c * LANES, (c + 1) * LANES)
            taps = [ext_ref[SUBLANES - 3 + t:SUBLANES - 3 + t + tm, cols] for t in range(CONV_W - 1)]
            act = _conv_silu(taps + [u_chunk[:, half * LANES:(half + 1) * LANES]], w, cols)
            _store_gdn_tile(act, c, gq_ref, gk_ref, gv_ref)

    wide = 2 * LANES

    def u_matmul(chunk):
        u_chunk = _dot(hb, w_ref[:, W_U0 + chunk * wide:W_U0 + (chunk + 1) * wide])
        ext_ref[SUBLANES:SUBLANES + tm, chunk * wide:(chunk + 1) * wide] = u_chunk
        return u_chunk

    u0 = u_matmul(0)
    a_q = _dot(hb, w_ref[:, W_Q0:W_K0])
    u1 = u_matmul(1)
    conv_tiles(0, u0)
    ms_q = _sub_head_mean_sq(a_q, gsum_ref)
    u2 = u_matmul(2)
    conv_tiles(1, u1)
    a_k = _dot(hb, w_ref[:, W_K0:W_V0])
    u3 = u_matmul(3)
    conv_tiles(2, u2)
    ms_k = _sub_head_mean_sq(a_k, gsum_ref)
    u4 = u_matmul(4)
    conv_tiles(3, u3)
    v = _dot(hb, w_ref[:, W_V0:W_U0])
    u5 = u_matmul(5)
    conv_tiles(4, u4)
    z = _dot(hb, w_ref[:, W_BG0:])
    conv_tiles(5, u5)
    conv_ref[0] = ext_ref[tm + SUBLANES - (CONV_W - 1):tm + SUBLANES, :]
    ext_ref[0:SUBLANES, :] = ext_ref[tm:tm + SUBLANES, :]
    q_heads = _norm_rope_heads(a_q, ms_q, qn_ref[...], cos, sin)
    for h in range(A_HEADS):
        q_ref[:, h * A_HD:(h + 1) * A_HD] = (q_heads[h] * Q_SCALE).astype(BF16)
    k_heads = _norm_rope_heads(a_k, ms_k, kn_ref[...], cos, sin)
    for h in range(A_HEADS):
        k_ref[pl.ds(h, tm, stride=A_HEADS), :] = k_heads[h]
        kb_ref[:, h * A_HD:(h + 1) * A_HD] = k_heads[h].astype(BF16)
    for h in range(A_HEADS):
        sl = slice(h * A_HD, (h + 1) * A_HD)
        v_ref[pl.ds(h, tm, stride=A_HEADS), :] = v[:, sl]
        vt_ref[0, 0, h] = v[:, sl].T.astype(BF16)
    bg_ref[...] = _beta_g(z, alog_ref[...], dtb_ref[...])


def _sample_proj_kernel(x_ref, wn_ref, w_ref, gsum_ref, qn_ref, kn_ref, cos_ref, sin_ref,
                        convw_ref, alog_ref, dtb_ref, c0_ref, c1_ref, c2_ref,
                        q_ref, k_ref, v_ref, gq_ref, gk_ref, gv_ref, bg_ref, u_ref):
    hb = _rms_rows(x_ref[...], wn_ref[...]).astype(BF16)
    cos = cos_ref[...]
    sin = sin_ref[...]
    a_q = _dot(hb, w_ref[:, W_Q0:W_K0])
    a_k = _dot(hb, w_ref[:, W_K0:W_V0])
    q_heads = _norm_rope_heads(a_q, _sub_head_mean_sq(a_q, gsum_ref), qn_ref[...], cos, sin)
    k_heads = _norm_rope_heads(a_k, _sub_head_mean_sq(a_k, gsum_ref), kn_ref[...], cos, sin)
    for h in range(A_HEADS):
        sl = slice(h * A_HD, (h + 1) * A_HD)
        q_ref[:, sl] = q_heads[h] * Q_SCALE
        k_ref[:, sl] = k_heads[h]
    v_ref[...] = _dot(hb, w_ref[:, W_V0:W_U0])
    u = _dot(hb, w_ref[:, W_U0:W_BG0])
    u_ref[...] = u
    w = convw_ref[...]

    def conv_tile(c):
        cols = slice(c * LANES, (c + 1) * LANES)
        return _conv_silu([c0_ref[:, cols], c1_ref[:, cols], c2_ref[:, cols], u[:, cols]], w, cols)

    for c in range(B_CONV // LANES):
        _store_gdn_tile(conv_tile(c), c, gq_ref, gk_ref, gv_ref)
    bg_ref[...] = _beta_g(_dot(hb, w_ref[:, W_BG0:]), alog_ref[...], dtb_ref[...])


def _full(shape):
    return pl.BlockSpec(shape, lambda *_: (0,) * len(shape))


def _prompt_projection(x2, n_seq, seq, wn, w_proj, gsum, qn, kn, cos, sin, convw, alog, dtb):
    rows = x2.shape[0]
    tm = min(PROJ_ROWS, seq)
    assert seq % tm == 0 and tm % LANES == 0
    tps = seq // tm
    d = x2.shape[1]
    row_spec = lambda width: pl.BlockSpec((tm, width), lambda i: (i, 0))
    head_row_spec = pl.BlockSpec((A_HEADS * tm, A_HD), lambda i: (i, 0))
    pos_spec = pl.BlockSpec((tm, LANES), lambda i: (i % tps, 0))
    out_shape = (
        jax.ShapeDtypeStruct((rows, A_WIDTH), BF16),
        jax.ShapeDtypeStruct((rows * A_HEADS, A_HD), F32),
        jax.ShapeDtypeStruct((rows, A_WIDTH), BF16),
        jax.ShapeDtypeStruct((rows * A_HEADS, A_HD), F32),
        jax.ShapeDtypeStruct((n_seq, tps, A_HEADS, A_HD, tm), BF16),
        jax.ShapeDtypeStruct((rows, B_QK), F32),
        jax.ShapeDtypeStruct((rows, B_QK), F32),
        jax.ShapeDtypeStruct((rows, B_WIDTH), F32),
        jax.ShapeDtypeStruct((rows, LANES), F32),
        jax.ShapeDtypeStruct((n_seq, CONV_W - 1, B_CONV), F32),
    )
    return pl.pallas_call(
        functools.partial(_prompt_proj_kernel, tiles_per_seq=tps),
        grid=(rows // tm,),
        in_specs=[row_spec(d), _full(wn.shape), _full(w_proj.shape), _full(gsum.shape),
                  _full(qn.shape), _full(kn.shape), pos_spec, pos_spec,
                  _full(convw.shape), _full(alog.shape), _full(dtb.shape)],
        out_specs=(row_spec(A_WIDTH), head_row_spec, row_spec(A_WIDTH), head_row_spec,
                   pl.BlockSpec((1, 1, A_HEADS, A_HD, tm), lambda i: (i // tps, i % tps, 0, 0, 0)),
                   row_spec(B_QK), row_spec(B_QK), row_spec(B_WIDTH), row_spec(LANES),
                   pl.BlockSpec((1, CONV_W - 1, B_CONV), lambda i: (i // tps, 0, 0))),
        out_shape=out_shape,
        scratch_shapes=[pltpu.VMEM((tm + SUBLANES, B_CONV), F32)],
        compiler_params=pltpu.CompilerParams(dimension_semantics=("arbitrary",),
                                             vmem_limit_bytes=VMEM_LIMIT_BYTES),
        name="prompt_projection",
    )(x2, wn, w_proj, gsum, qn, kn, cos, sin, convw, alog, dtb)


def _sample_projection(x2, wn, w_proj, gsum, qn, kn, cos, sin, convw, alog, dtb, c0, c1, c2):
    rows = x2.shape[0]
    out_shape = (
        jax.ShapeDtypeStruct((rows, A_WIDTH), F32),
        jax.ShapeDtypeStruct((rows, A_WIDTH), F32),
        jax.ShapeDtypeStruct((rows, A_WIDTH), F32),
        jax.ShapeDtypeStruct((rows, B_QK), F32),
        jax.ShapeDtypeStruct((rows, B_QK), F32),
        jax.ShapeDtypeStruct((rows, B_WIDTH), F32),
        jax.ShapeDtypeStruct((rows, LANES), F32),
        jax.ShapeDtypeStruct((rows, B_CONV), F32),
    )
    args = (x2, wn, w_proj, gsum, qn, kn, cos, sin, convw, alog, dtb, c0, c1, c2)
    return pl.pallas_call(
        _sample_proj_kernel,
        grid=(1,),
        in_specs=[_full(a.shape) for a in args],
        out_specs=tuple(_full(s.shape) for s in out_shape),
        out_shape=out_shape,
        compiler_params=pltpu.CompilerParams(dimension_semantics=("arbitrary",),
                                             vmem_limit_bytes=VMEM_LIMIT_BYTES),
        name="sample_projection",
    )(*args)


def _lambda_value(lamp, lam_init):
    s1 = jnp.sum(lamp[0:1, :] * lamp[1:2, :], axis=1, keepdims=True)
    s2 = jnp.sum(lamp[2:3, :] * lamp[3:4, :], axis=1, keepdims=True)
    return jnp.exp(s1) - jnp.exp(s2) + lam_init


def _dual_attn_kernel(pt_ref, tn_ref, th_ref, tqb_ref, tkv_ref,
                      lamp_ref, subln_ref, q_ref, k_ref, vt_ref, q8_ref, k8_ref, v8_ref, *rest,
                      pages_per_step, paged_steps, chunks_per_seq, lam_init):
    del tn_ref, th_ref
    (ck_ref, cv_ref, o_ref, os_ref, m_ref, acc_ref, sm_ref, sl_ref, sacc_ref,
     kbuf_ref, vbuf_ref, ksem_ref, vsem_ref) = rest
    step = pl.program_id(0)
    qb = tqb_ref[step]
    kv = tkv_ref[step]
    tq = q_ref.shape[1]
    lam = _lambda_value(lamp_ref[...], lam_init)
    subln = subln_ref[...]
    active = step < paged_steps
    n_buf = kbuf_ref.shape[0]
    slot = lax.rem(step, n_buf)

    def page_copies(page_step):
        buf = lax.rem(page_step, n_buf)
        copies = []
        for j in range(pages_per_step):
            page = pt_ref[page_step * pages_per_step + j]
            copies.append(pltpu.make_async_copy(ck_ref.at[page], kbuf_ref.at[buf, j], ksem_ref.at[buf]))
            copies.append(pltpu.make_async_copy(cv_ref.at[page], vbuf_ref.at[buf, j], vsem_ref.at[buf]))
        return copies

    @pl.when(step == 0)
    def _():
        for ahead in range(min(n_buf - 1, paged_steps)):
            for c in page_copies(ahead):
                c.start()

    @pl.when(step + (n_buf - 1) < paged_steps)
    def _():
        for c in page_copies(step + (n_buf - 1)):
            c.start()

    @pl.when(active)
    def _():
        for c in page_copies(step):
            c.wait()

    @pl.when(kv == 0)
    def _():
        m_ref[...] = jnp.full(m_ref.shape, -jnp.inf, F32)
        acc_ref[...] = jnp.zeros_like(acc_ref)

    chunk = lax.rem(step, chunks_per_seq)
    rows = 2 * A_HEADS
    page_rows = PAGE_SIZE * A_HEADS
    qrow = lax.broadcasted_iota(jnp.int32, (rows, A_HD), 0)
    qlane = lax.broadcasted_iota(jnp.int32, (rows, A_HD), 1)
    qmat = jnp.where(qlane // A_DH == qrow // A_HEADS, q8_ref[0], 0.0)

    @pl.when(jnp.logical_and(active, chunk == 0))
    def _():
        sm_ref[...] = jnp.sum(qmat * k8_ref[0], axis=1, keepdims=True)
        sl_ref[...] = jnp.ones_like(sl_ref)
        sacc_ref[...] = v8_ref[0]

    def dual_step(masked):
        q = q_ref[0]
        lane = lax.broadcasted_iota(jnp.int32, q.shape, 1)
        zero = jnp.zeros_like(q)
        kblk = k_ref[0]
        s_half = [_dot_nt(kblk, jnp.where(lane < A_DH, q, zero)),
                  _dot_nt(kblk, jnp.where(lane >= A_DH, q, zero))]
        qmb = qmat.astype(BF16)
        sc = jnp.concatenate([_dot_nt(qmb, kbuf_ref[slot, j].astype(BF16))
                              for j in range(pages_per_step)], axis=1)
        if masked:
            causal = (lax.broadcasted_iota(jnp.int32, (tq, tq), 0)
                      <= lax.broadcasted_iota(jnp.int32, (tq, tq), 1))
        tv = vt_ref.shape[-1]
        ones_rows = jnp.ones((acc_ref.shape[0] - A_HD, tv), BF16)
        for half in range(2):
            cols = slice(half * tq, (half + 1) * tq)
            s = jnp.where(causal, s_half[half], -jnp.inf) if masked else s_half[half]
            m_prev = m_ref[:, cols]
            m_new = jnp.maximum(m_prev, jnp.max(s, axis=0, keepdims=True))
            alpha = jnp.exp2(m_prev - m_new)
            pb = jnp.exp2(s - m_new).astype(BF16)
            m_ref[:, cols] = m_new
            acc = alpha * acc_ref[:, cols]
            for j in range(vt_ref.shape[1]):
                vt_ext = jnp.concatenate([vt_ref[0, j, 0], ones_rows], axis=0)
                acc = acc + _dot(vt_ext, pb[j * tv:(j + 1) * tv])
            acc_ref[:, cols] = acc
        row = lax.broadcasted_iota(jnp.int32, sc.shape, 0)
        col = lax.broadcasted_iota(jnp.int32, sc.shape, 1)
        sc = jnp.where(col % A_HEADS == row % A_HEADS, sc, -jnp.inf)
        sm_prev = sm_ref[...]
        sl_prev = sl_ref[...]
        sacc_prev = sacc_ref[...]
        sm_new = jnp.maximum(sm_prev, jnp.max(sc, axis=1, keepdims=True))
        salpha = jnp.exp2(sm_prev - sm_new)
        sp = jnp.exp2(sc - sm_new)
        spb = sp.astype(BF16)
        pv = _dot(spb[:, 0:page_rows], vbuf_ref[slot, 0].astype(BF16))
        for j in range(1, pages_per_step):
            pv = pv + _dot(spb[:, j * page_rows:(j + 1) * page_rows], vbuf_ref[slot, j].astype(BF16))
        sl_ref[...] = jnp.where(active, salpha * sl_prev + jnp.sum(sp, axis=1, keepdims=True), sl_prev)
        sacc_ref[...] = jnp.where(active, salpha * sacc_prev + pv, sacc_prev)
        sm_ref[...] = jnp.where(active, sm_new, sm_prev)

    @pl.when(kv < qb)
    def _():
        dual_step(False)

    @pl.when(kv == qb)
    def _():
        dual_step(True)
        o = acc_ref[0:A_HD, :] / acc_ref[A_HD:A_HD + 1, :]
        out = (o[:, :tq] - lam * o[:, tq:]).T
        o_ref[0] = _rms_rows(out, subln) * (1.0 - lam_init)

    @pl.when(jnp.logical_and(active, chunk == chunks_per_seq - 1))
    def _():
        o = sacc_ref[...] / sl_ref[...]
        out = o[:A_HEADS] - lam * o[A_HEADS:]
        os_ref[0] = _rms_rows(out, subln) * (1.0 - lam_init)


def _dual_attention(q, kb, vt, q_s, k_s, v_s, cache_k, cache_v, page_table, lamp, subln,
                    n_seq, seq, lam_init):
    tv = vt.shape[-1]
    tq = min(ATTN_BLOCK, seq)
    assert seq % tq == 0 and tq % tv == 0
    nq = seq // tq
    n_dec, n_pages = page_table.shape
    n_pool = cache_k.shape[0]
    pairs = [(n, h, i, j) for n in range(n_seq) for h in range(A_HEADS)
             for i in range(nq) for j in range(i + 1)]
    steps = len(pairs)
    fits = [p for p in range(1, n_pages + 1) if n_pages % p == 0 and n_dec * (n_pages // p) <= steps]
    assert fits, "prompt attention grid too short to carry the cache page stream"
    pps = fits[0]
    cps = n_pages // pps
    paged_steps = n_dec * cps
    tabs = [jnp.asarray([p[c] for p in pairs], jnp.int32) for c in range(4)]

    q3 = q.reshape(n_seq, seq, A_WIDTH)
    k3 = kb.reshape(n_seq, seq, A_WIDTH)
    ck = cache_k.reshape(n_pool, PAGE_SIZE * A_HEADS, A_HD)
    cv = cache_v.reshape(n_pool, PAGE_SIZE * A_HEADS, A_HD)
    stack2 = lambda a: jnp.tile(a.reshape(n_dec, A_HEADS, A_HD), (1, 2, 1))
    rows = 2 * A_HEADS

    pstep = lambda s: jnp.minimum(s, paged_steps - 1)
    const = lambda shape: pl.BlockSpec(shape, lambda s, *_: (0,) * len(shape))
    q_spec = pl.BlockSpec((1, tq, A_HD), lambda s, pt, tn, th, tqb, tkv: (tn[s], tqb[s], th[s]))
    k_spec = pl.BlockSpec((1, tq, A_HD), lambda s, pt, tn, th, tqb, tkv: (tn[s], tkv[s], th[s]))
    vt_spec = pl.BlockSpec((1, tq // tv, 1, A_HD, tv),
                           lambda s, pt, tn, th, tqb, tkv: (tn[s], tkv[s], th[s], 0, 0))
    row_spec = pl.BlockSpec((1, rows, A_HD), lambda s, *_: (pstep(s) // cps, 0, 0))
    hbm_spec = pl.BlockSpec(memory_space=pl.ANY)
    page_buf = pltpu.VMEM((PAGE_RING, pps, PAGE_SIZE * A_HEADS, A_HD), F32)

    grid_spec = pltpu.PrefetchScalarGridSpec(
        num_scalar_prefetch=5,
        grid=(steps,),
        in_specs=[const(lamp.shape), const(subln.shape), q_spec, k_spec, vt_spec,
                  row_spec, row_spec, row_spec, hbm_spec, hbm_spec],
        out_specs=(q_spec, pl.BlockSpec((1, A_HEADS, A_HD), lambda s, *_: (pstep(s) // cps, 0, 0))),
        scratch_shapes=[pltpu.VMEM((1, 2 * tq), F32),
                        pltpu.VMEM((A_HD + DENOM_ROWS, 2 * tq), F32),
                        pltpu.VMEM((rows, 1), F32), pltpu.VMEM((rows, 1), F32),
                        pltpu.VMEM((rows, A_HD), F32),
                        page_buf, page_buf,
                        pltpu.SemaphoreType.DMA((PAGE_RING,)), pltpu.SemaphoreType.DMA((PAGE_RING,))],
    )
    o_p, o_s = pl.pallas_call(
        functools.partial(_dual_attn_kernel, pages_per_step=pps, paged_steps=paged_steps,
                          chunks_per_seq=cps, lam_init=lam_init),
        grid_spec=grid_spec,
        out_shape=(jax.ShapeDtypeStruct((n_seq, seq, A_WIDTH), F32),
                   jax.ShapeDtypeStruct((n_dec, A_HEADS, A_HD), F32)),
        compiler_params=pltpu.CompilerParams(dimension_semantics=("arbitrary",),
                                             vmem_limit_bytes=VMEM_LIMIT_BYTES),
        name="dual_attention",
    )(page_table.reshape(-1), *tabs, lamp, subln, q3, k3, vt, stack2(q_s), stack2(k_s), stack2(v_s), ck, cv)
    return o_p.reshape(n_seq * seq, A_WIDTH), o_s.reshape(n_dec, A_WIDTH)


def _gdn_chunk_kernel(gq_ref, gk_ref, gv_ref, bg_ref, onorm_ref, o_ref, s_out_ref, s_ref):
    tb = pl.program_id(0)
    n_seq, tt, _ = gq_ref.shape
    nc = tt // CHUNK

    @pl.when(tb == 0)
    def _():
        s_ref[...] = jnp.zeros_like(s_ref)

    ii = lax.broadcasted_iota(jnp.int32, (CHUNK, CHUNK), 0)
    jj = lax.broadcasted_iota(jnp.int32, (CHUNK, CHUNK), 1)
    incl = ii >= jj
    strict = ii > jj
    ltri = incl.astype(BF16)
    onorm = onorm_ref[...]

    heads = [(n, h) for n in range(n_seq) for h in range(B_HEADS)]

    def chunk_step(ci, carry):
        r0 = pl.multiple_of(ci * CHUNK, CHUNK)
        rows = pl.ds(r0, CHUNK)
        head_cols = lambda ref: jnp.stack([ref[n, rows, h * B_DK:(h + 1) * B_DK] for n, h in heads])
        q = head_cols(gq_ref)
        k = head_cols(gk_ref)
        v = head_cols(gv_ref)
        bgc = [bg_ref[n, rows, :] for n in range(n_seq)]
        gcum = [_dot_exact_lhs(ltri, b) for b in bgc]
        lane_bcast = lambda x, c: jnp.broadcast_to(x[:, c:c + 1], (CHUNK, B_DK))
        bh = jnp.stack([lane_bcast(bgc[n], h) for n, h in heads])
        gcs = [lane_bcast(gcum[n], B_HEADS + h) for n, h in heads]
        gc = jnp.stack(gcs)
        d = gc[:, :, :CHUNK] - jnp.stack([g.T[:CHUNK, :] for g in gcs])
        decay = jnp.exp(jnp.where(incl[None], d, -jnp.inf))
        glast = gc[:, CHUNK - 1:CHUNK, :]
        eg = jnp.exp(gc)
        kb = k * bh
        kbf = k.astype(BF16)
        m = -jnp.where(strict[None], _bmm_nt(kb.astype(BF16), kbf) * decay, 0.0)
        sol = jnp.concatenate([v * bh, kb * eg], axis=2)
        mm = lambda level, a, b: (_bmm_split(a, b) if level < SOLVE_SPLIT_LEVELS
                                  else _bmm(a.astype(BF16), b.astype(BF16)))
        sol = sol + mm(0, m, sol)
        p = m
        for level in range(5):
            p = mm(level, p, p)
            sol = sol + mm(level + 1, p, sol)
        u = sol[:, :, :B_DV]
        w = sol[:, :, B_DV:]
        qk = jnp.where(incl[None], _bmm_nt(q.astype(BF16), kbf) * decay, 0.0)
        s = s_ref[...]
        sb = s.astype(BF16)
        vn = (u - _bmm(w.astype(BF16), sb)).astype(BF16)
        o = _bmm((q * eg).astype(BF16), sb) + _bmm(qk.astype(BF16), vn)
        kg = (k * jnp.exp(glast - gc)).astype(BF16)
        s_ref[...] = s * jnp.exp(glast) + _bmm_tn(kg, vn)
        on = _rms_rows(o, onorm)
        for i, (n, h) in enumerate(heads):
            o_ref[n, rows, h * B_DV:(h + 1) * B_DV] = on[i]
        return carry

    lax.fori_loop(0, nc, chunk_step, 0)

    @pl.when(tb == pl.num_programs(0) - 1)
    def _():
        s_out_ref[...] = s_ref[...]


def _gdn_chunked(gq, gk, gv, bg, onorm, n_seq, seq):
    tt = min(GDN_ROWS, seq)
    assert seq % tt == 0 and tt % CHUNK == 0
    blk = lambda width: pl.BlockSpec((n_seq, tt, width), lambda t: (0, t, 0))
    state_shape = (n_seq * B_HEADS, B_DK, B_DV)
    out_shape = (jax.ShapeDtypeStruct((n_seq, seq, B_WIDTH), F32),
                 jax.ShapeDtypeStruct(state_shape, F32))
    o, s = pl.pallas_call(
        _gdn_chunk_kernel,
        grid=(seq // tt,),
        in_specs=[blk(B_QK), blk(B_QK), blk(B_WIDTH), blk(LANES), _full(onorm.shape)],
        out_specs=(blk(B_WIDTH), _full(state_shape)),
        out_shape=out_shape,
        scratch_shapes=[pltpu.VMEM(state_shape, F32)],
        compiler_params=pltpu.CompilerParams(dimension_semantics=("arbitrary",),
                                             vmem_limit_bytes=VMEM_LIMIT_BYTES),
        name="gdn_chunked",
    )(gq.reshape(n_seq, seq, B_QK), gk.reshape(n_seq, seq, B_QK), gv.reshape(n_seq, seq, B_WIDTH),
      bg.reshape(n_seq, seq, LANES), onorm)
    return o.reshape(n_seq * seq, B_WIDTH), s.reshape(n_seq, B_HEADS, B_DK, B_DV)


def _gdn_step_kernel(gq_ref, gk_ref, gv_ref, bg_ref, onorm_ref, s_ref, o_ref, so_ref):
    ri = lax.broadcasted_iota(jnp.int32, (B_DK, B_DK), 0)
    ci = lax.broadcasted_iota(jnp.int32, (B_DK, B_DK), 1)
    eye = (ri == ci).astype(F32)
    onorm = onorm_ref[...]
    for b in range(gq_ref.shape[0]):
        bg = bg_ref[b]
        for h in range(B_HEADS):
            sl = slice(h * B_DK, (h + 1) * B_DK)
            q = gq_ref[b][:, sl]
            k = gk_ref[b][:, sl]
            v = gv_ref[b][:, sl]
            beta = bg[:, h:h + 1]
            a = jnp.exp(bg[:, B_HEADS + h:B_HEADS + h + 1])
            kcol = jnp.sum(eye * k, axis=1, keepdims=True)
            qcol = jnp.sum(eye * q, axis=1, keepdims=True)
            s = s_ref[b, h] * a
            u = beta * (v - jnp.sum(s * kcol, axis=0, keepdims=True))
            s = s + kcol * u
            so_ref[b, h] = s
            o = jnp.sum(s * qcol, axis=0, keepdims=True)
            o_ref[b, :, sl] = _rms_rows(o, onorm)


def _gdn_step(gq, gk, gv, bg, onorm, state):
    n_dec = state.shape[0]
    bb = max(p for p in range(1, GDN_STEP_SEQS + 1) if n_dec % p == 0)
    row_spec = lambda width: pl.BlockSpec((bb, 1, width), lambda i: (i, 0, 0))
    s_spec = pl.BlockSpec((bb, B_HEADS, B_DK, B_DV), lambda i: (i, 0, 0, 0))
    out_shape = (jax.ShapeDtypeStruct((n_dec, 1, B_WIDTH), F32),
                 jax.ShapeDtypeStruct(state.shape, F32))
    o, s_new = pl.pallas_call(
        _gdn_step_kernel,
        grid=(n_dec // bb,),
        in_specs=[row_spec(B_QK), row_spec(B_QK), row_spec(B_WIDTH), row_spec(LANES),
                  pl.BlockSpec(onorm.shape, lambda i: (0, 0)), s_spec],
        out_specs=(row_spec(B_WIDTH), s_spec),
        out_shape=out_shape,
        compiler_params=pltpu.CompilerParams(dimension_semantics=("arbitrary",),
                                             vmem_limit_bytes=VMEM_LIMIT_BYTES),
        name="gdn_step",
    )(gq.reshape(n_dec, 1, B_QK), gk.reshape(n_dec, 1, B_QK), gv.reshape(n_dec, 1, B_WIDTH),
      bg.reshape(n_dec, 1, LANES), onorm, state.astype(F32))
    return o.reshape(n_dec, B_WIDTH), s_new


def _merge_out_kernel(x_ref, wn_ref, wg_ref, oa_ref, ob_ref, wua_ref, wub_ref, wo_ref, y_ref):
    x = x_ref[...]
    hb = _rms_rows(x, wn_ref[...]).astype(BF16)
    az = _dot(hb, wg_ref[:, 0:A_WIDTH])
    bz = _dot(hb, wg_ref[:, A_WIDTH:A_WIDTH + B_WIDTH])
    oa = oa_ref[...] * (az * _sigmoid(az))
    ob = ob_ref[...] * (bz * _sigmoid(bz))
    d = x.shape[1]
    g0 = A_WIDTH + B_WIDTH
    ga = _dot(hb, wg_ref[:, g0:g0 + d])
    gb = _dot(hb, wg_ref[:, g0 + d:g0 + 2 * d])
    y = (_sigmoid(ga) * _dot(oa.astype(BF16), wua_ref[...])
         + _sigmoid(gb) * _dot(ob.astype(BF16), wub_ref[...]))
    y_ref[...] = x + _dot(y.astype(BF16), wo_ref[...])


def _merge_out(x2, wn, w_gate, o_a, o_b, w_up_a, w_up_b, w_out, tm):
    rows, d = x2.shape
    assert rows % tm == 0
    row_spec = lambda width: pl.BlockSpec((tm, width), lambda i: (i, 0))
    return pl.pallas_call(
        _merge_out_kernel,
        grid=(rows // tm,),
        in_specs=[row_spec(d), _full(wn.shape), _full(w_gate.shape), row_spec(A_WIDTH), row_spec(B_WIDTH),
                  _full(w_up_a.shape), _full(w_up_b.shape), _full(w_out.shape)],
        out_specs=row_spec(d),
        out_shape=jax.ShapeDtypeStruct((rows, d), F32),
        compiler_params=pltpu.CompilerParams(dimension_semantics=("arbitrary",),
                                             vmem_limit_bytes=VMEM_LIMIT_BYTES),
        name="merge_out",
    )(x2, wn, w_gate, o_a, o_b, w_up_a, w_up_b, w_out)


def _rope_tables(pos):
    half = A_DH // 2
    inv = ROPE_THETA ** (-jnp.arange(half, dtype=F32) / half)
    ang = pos.astype(F32)[:, None] * inv[None, :]
    cos = jnp.cos(ang)
    sin = jnp.sin(ang)
    return jnp.tile(cos, (1, 4)), jnp.concatenate([-sin, sin, -sin, sin], axis=1)


def _repack_w_in(w):
    d = w.shape[0]
    o = 0
    parts = {}
    for name, size in (("aq", A_WIDTH), ("ak", A_WIDTH), ("av", A_WIDTH), ("az", A_WIDTH),
                       ("bq", B_QK), ("bk", B_QK), ("bv", B_WIDTH), ("bz", B_WIDTH),
                       ("bb", B_HEADS), ("ba", B_HEADS), ("ga", d), ("gb", d)):
        parts[name] = w[:, o:o + size]
        o += size
    assert o == w.shape[1]
    pad = jnp.zeros((d, LANES - 2 * B_HEADS), w.dtype)
    w_proj = jnp.concatenate([parts[n] for n in ("aq", "ak", "av", "bq", "bk", "bv", "bb", "ba")] + [pad], axis=1)
    w_gate = jnp.concatenate([parts[n] for n in ("az", "bz", "ga", "gb")], axis=1)
    return w_proj.astype(BF16), w_gate.astype(BF16)


def _lane_row(vec, offset):
    return jnp.zeros((1, LANES), F32).at[0, offset:offset + vec.shape[0]].set(vec.astype(F32))


def kernel(x_prompt, x_sample, cache_k, cache_v, state_conv, state_ssm, page_table, w_norm, w_in, a_qn, a_kn, a_lq1, a_lk1, a_lq2, a_lk2, a_subln, conv_w, a_log, dt_bias, b_onorm, w_up_a, w_up_b, w_out):
    depth = w_norm.shape[0]
    n_seq, seq, d = x_prompt.shape
    n_dec, dec_seq, _ = x_sample.shape
    assert dec_seq == 1, "the sample path handles one new token per sequence"
    n_pages = page_table.shape[1]
    past_len = n_pages * PAGE_SIZE

    cos_p, sin_p = _rope_tables(jnp.arange(seq, dtype=jnp.int32))
    cos_s, sin_s = _rope_tables(past_len + jnp.arange(dec_seq, dtype=jnp.int32))
    sub = lax.broadcasted_iota(jnp.int32, (A_WIDTH, A_WIDTH), 0) // A_DH
    gsum = jnp.where(sub == sub.T, 1.0 / A_DH, 0.0).astype(BF16)

    hp = x_prompt.reshape(n_seq * seq, d)
    hs = x_sample.reshape(n_dec, d)
    outs = [[] for _ in range(8)]
    for l in range(depth):
        lam_init = 0.8 - 0.6 * math.exp(-0.3 * l)
        wn = w_norm[l].reshape(1, d)
        w_proj, w_gate = _repack_w_in(w_in[l])
        qn = jnp.tile(a_qn[l], A_WIDTH // A_DH).reshape(1, A_WIDTH)
        kn = jnp.tile(a_kn[l], A_WIDTH // A_DH).reshape(1, A_WIDTH)
        lamp = jnp.stack([a_lq1[l], a_lk1[l], a_lq2[l], a_lk2[l]]).astype(F32)
        subln = a_subln[l].reshape(1, A_HD)
        alog = _lane_row(a_log[l], B_HEADS)
        dtb = _lane_row(dt_bias[l], B_HEADS)
        onorm = b_onorm[l].reshape(1, B_DV)
        wua = w_up_a[l].astype(BF16)
        wub = w_up_b[l].astype(BF16)
        wo = w_out[l].astype(BF16)
        convw = conv_w[l]

        q, k, kb, v, vt, gq, gk, gv, bg, conv_p = _prompt_projection(
            hp, n_seq, seq, wn, w_proj, gsum, qn, kn, cos_p, sin_p, convw, alog, dtb)
        sc = state_conv[l]
        qs, ks, vs, gqs, gks, gvs, bgs, us = _sample_projection(
            hs, wn, w_proj, gsum, qn, kn, cos_s, sin_s, convw, alog, dtb,
            sc[:, 0, :], sc[:, 1, :], sc[:, 2, :])
        o_a, o_as = _dual_attention(q, kb, vt, qs, ks, vs, cache_k[l], cache_v[l], page_table,
                                    lamp, subln, n_seq, seq, lam_init)
        o_b, ssm_p = _gdn_chunked(gq, gk, gv, bg, onorm, n_seq, seq)
        hp = _merge_out(hp, wn, w_gate, o_a, o_b, wua, wub, wo, min(MERGE_ROWS, seq))
        o_bs, ssm_s = _gdn_step(gqs, gks, gvs, bgs, onorm, state_ssm[l])
        hs = _merge_out(hs, wn, w_gate, o_as, o_bs, wua, wub, wo, n_dec)

        outs[0].append(k.reshape(n_seq, seq, A_HEADS, A_HD))
        outs[1].append(v.reshape(n_seq, seq, A_HEADS, A_HD))
        outs[2].append(conv_p)
        outs[3].append(ssm_p.astype(x_prompt.dtype))
        outs[4].append(ks.reshape(n_dec, dec_seq, A_HEADS, A_HD))
        outs[5].append(vs.reshape(n_dec, dec_seq, A_HEADS, A_HD))
        outs[6].append(jnp.concatenate([sc[:, 1:, :], us[:, None, :]], axis=1))
        outs[7].append(ssm_s.astype(x_sample.dtype))

    return (hp.reshape(n_seq, seq, d), hs.reshape(n_dec, dec_seq, d)) + tuple(jnp.stack(o) for o in outs)
```

```python
import functools
import math

import jax
import jax.numpy as jnp
from jax import lax
from jax.experimental import pallas as pl
from jax.experimental.pallas import tpu as pltpu

F32 = jnp.float32
BF16 = jnp.bfloat16

A_HEADS = 4
A_DH = 64
A_HD = 2 * A_DH
A_WIDTH = A_HEADS * A_HD
B_HEADS = 4
B_DK = 128
B_DV = 128
B_QK = B_HEADS * B_DK
B_WIDTH = B_HEADS * B_DV
B_CONV = 2 * B_QK + B_WIDTH
CONV_W = 4
CHUNK = 64
PAGE_SIZE = 128
ROPE_THETA = 10000.0
EPS = 1e-6

LANES = 128
SUBLANES = 8
VMEM_LIMIT_BYTES = 56 * 1024 * 1024

PROJ_ROWS = 256
ATTN_BLOCK = 512
ATTN_KEY_BLOCK = 1024
MERGE_ROWS = 256
GDN_ROWS = 256
PAGE_RING = 3
GDN_STEP_SEQS = 4
SOLVE_SPLIT_LEVELS = 3
DENOM_ROWS = 16
Q_SCALE = (A_DH ** -0.5) * math.log2(math.e)


def _dot(a, b):
    return jnp.dot(a, b, preferred_element_type=F32)


def _dot_nt(a, b):
    return lax.dot_general(a, b, (((1,), (1,)), ((), ())), preferred_element_type=F32)


def _bmm(a, b):
    return lax.dot_general(a, b, (((2,), (1,)), ((0,), (0,))), preferred_element_type=F32)


def _bmm_split(a, b):
    ah = a.astype(BF16)
    al = (a - ah.astype(F32)).astype(BF16)
    bh = b.astype(BF16)
    bl = (b - bh.astype(F32)).astype(BF16)
    return _bmm(ah, bh) + _bmm(ah, bl) + _bmm(al, bh)


def _bmm_nt(a, b):
    return lax.dot_general(a, b, (((2,), (2,)), ((0,), (0,))), preferred_element_type=F32)


def _bmm_tn(a, b):
    return lax.dot_general(a, b, (((1,), (1,)), ((0,), (0,))), preferred_element_type=F32)


def _split3(x):
    x1 = x.astype(BF16)
    r1 = x - x1.astype(F32)
    x2 = r1.astype(BF16)
    x3 = (r1 - x2.astype(F32)).astype(BF16)
    return x1, x2, x3


def _dot_exact_lhs(mat01, x):
    x1, x2, x3 = _split3(x)
    return _dot(mat01, x1) + _dot(mat01, x2) + _dot(mat01, x3)


def _sigmoid(x):
    return 1.0 / (1.0 + jnp.exp(-x))


def _softplus(x):
    return jnp.maximum(x, 0.0) + jnp.log1p(jnp.exp(-jnp.abs(x)))


def _rms_rows(x, gain):
    return x * lax.rsqrt(jnp.mean(x * x, axis=-1, keepdims=True) + EPS) * gain


def _sub_head_mean_sq(a, gsum_ref):
    return _dot((a * a).astype(BF16), gsum_ref[...])


def _norm_rope_heads(a, ms, gain, cos, sin):
    y = a * lax.rsqrt(ms + EPS) * gain
    lane = lax.broadcasted_iota(jnp.int32, (a.shape[0], LANES), 1)
    first_half = (lane % A_DH) < (A_DH // 2)
    heads = []
    for j in range(A_HEADS):
        t = y[:, j * A_HD:(j + 1) * A_HD]
        partner = jnp.where(first_half,
                            pltpu.roll(t, LANES - A_DH // 2, 1),
                            pltpu.roll(t, A_DH // 2, 1))
        heads.append(t * cos + partner * sin)
    return heads


def _conv_silu(taps, w, cols):
    y = taps[0] * w[0:1, cols]
    for i in range(1, CONV_W):
        y = y + taps[i] * w[i:i + 1, cols]
    return y * _sigmoid(y)


def _store_gdn_tile(t, c, gq_ref, gk_ref, gv_ref):
    group, h = divmod(c, B_HEADS)
    if group < 2:
        t = t * lax.rsqrt(jnp.sum(t * t, axis=-1, keepdims=True) + EPS)
    if group == 0:
        t = t * (B_DK ** -0.5)
    (gq_ref, gk_ref, gv_ref)[group][:, h * B_DK:(h + 1) * B_DK] = t


def _beta_g(z, alog_row, dtb_row):
    lane = lax.broadcasted_iota(jnp.int32, z.shape, 1)
    beta = _sigmoid(z)
    g = -jnp.exp(alog_row) * _softplus(z + dtb_row)
    return jnp.where(lane < B_HEADS, beta, jnp.where(lane < 2 * B_HEADS, g, 0.0))


W_Q0, W_K0, W_V0, W_U0, W_BG0 = 0, A_WIDTH, 2 * A_WIDTH, 3 * A_WIDTH, 3 * A_WIDTH + B_CONV


def _prompt_proj_kernel(x_ref, wn_ref, w_ref, gsum_ref, qn_ref, kn_ref, cos_ref, sin_ref,
                        convw_ref, alog_ref, dtb_ref,
                        q_ref, k_ref, kb_ref, v_ref, vt_ref, gq_ref, gk_ref, gv_ref, bg_ref, conv_ref,
                        ext_ref, *, tiles_per_seq):
    i = pl.program_id(0)
    tm = x_ref.shape[0]
    hb = _rms_rows(x_ref[...], wn_ref[...]).astype(BF16)
    cos = cos_ref[...]
    sin = sin_ref[...]
    w = convw_ref[...]

    @pl.when(i % tiles_per_seq == 0)
    def _():
        ext_ref[0:SUBLANES, :] = jnp.zeros((SUBLANES, B_CONV), F32)

    def conv_tiles(chunk, u_chunk):
        for half in range(2):
            c = 2 * chunk + half
            cols = slice(c * LANES, (c + 1) * LANES)
            taps = [ext_ref[SUBLANES - 3 + t:SUBLANES - 3 + t + tm, cols] for t in range(CONV_W - 1)]
            act = _conv_silu(taps + [u_chunk[:, half * LANES:(half + 1) * LANES]], w, cols)
            _store_gdn_tile(act, c, gq_ref, gk_ref, gv_ref)

    wide = 2 * LANES

    def u_matmul(chunk):
        u_chunk = _dot(hb, w_ref[:, W_U0 + chunk * wide:W_U0 + (chunk + 1) * wide])
        ext_ref[SUBLANES:SUBLANES + tm, chunk * wide:(chunk + 1) * wide] = u_chunk
        return u_chunk

    u0 = u_matmul(0)
    a_q = _dot(hb, w_ref[:, W_Q0:W_K0])
    u1 = u_matmul(1)
    conv_tiles(0, u0)
    ms_q = _sub_head_mean_sq(a_q, gsum_ref)
    u2 = u_matmul(2)
    conv_tiles(1, u1)
    a_k = _dot(hb, w_ref[:, W_K0:W_V0])
    u3 = u_matmul(3)
    conv_tiles(2, u2)
    ms_k = _sub_head_mean_sq(a_k, gsum_ref)
    u4 = u_matmul(4)
    conv_tiles(3, u3)
    v = _dot(hb, w_ref[:, W_V0:W_U0])
    u5 = u_matmul(5)
    conv_tiles(4, u4)
    z = _dot(hb, w_ref[:, W_BG0:])
    conv_tiles(5, u5)
    conv_ref[0] = ext_ref[tm + SUBLANES - (CONV_W - 1):tm + SUBLANES, :]
    ext_ref[0:SUBLANES, :] = ext_ref[tm:tm + SUBLANES, :]
    q_heads = _norm_rope_heads(a_q, ms_q, qn_ref[...], cos, sin)
    for h in range(A_HEADS):
        q_ref[:, h * A_HD:(h + 1) * A_HD] = (q_heads[h] * Q_SCALE).astype(BF16)
    k_heads = _norm_rope_heads(a_k, ms_k, kn_ref[...], cos, sin)
    for h in range(A_HEADS):
        k_ref[pl.ds(h, tm, stride=A_HEADS), :] = k_heads[h]
        kb_ref[:, h * A_HD:(h + 1) * A_HD] = k_heads[h].astype(BF16)
    for h in range(A_HEADS):
        sl = slice(h * A_HD, (h + 1) * A_HD)
        v_ref[pl.ds(h, tm, stride=A_HEADS), :] = v[:, sl]
        vt_ref[0, 0, h] = v[:, sl].T.astype(BF16)
    bg_ref[...] = _beta_g(z, alog_ref[...], dtb_ref[...])


def _sample_proj_kernel(x_ref, wn_ref, w_ref, gsum_ref, qn_ref, kn_ref, cos_ref, sin_ref,
                        convw_ref, alog_ref, dtb_ref, c0_ref, c1_ref, c2_ref,
                        q_ref, k_ref, v_ref, gq_ref, gk_ref, gv_ref, bg_ref, u_ref):
    hb = _rms_rows(x_ref[...], wn_ref[...]).astype(BF16)
    cos = cos_ref[...]
    sin = sin_ref[...]
    a_q = _dot(hb, w_ref[:, W_Q0:W_K0])
    a_k = _dot(hb, w_ref[:, W_K0:W_V0])
    q_heads = _norm_rope_heads(a_q, _sub_head_mean_sq(a_q, gsum_ref), qn_ref[...], cos, sin)
    k_heads = _norm_rope_heads(a_k, _sub_head_mean_sq(a_k, gsum_ref), kn_ref[...], cos, sin)
    for h in range(A_HEADS):
        sl = slice(h * A_HD, (h + 1) * A_HD)
        q_ref[:, sl] = q_heads[h] * Q_SCALE
        k_ref[:, sl] = k_heads[h]
    v_ref[...] = _dot(hb, w_ref[:, W_V0:W_U0])
    u = _dot(hb, w_ref[:, W_U0:W_BG0])
    u_ref[...] = u
    w = convw_ref[...]

    def conv_tile(c):
        cols = slice(c * LANES, (c + 1) * LANES)
        return _conv_silu([c0_ref[:, cols], c1_ref[:, cols], c2_ref[:, cols], u[:, cols]], w, cols)

    for c in range(B_CONV // LANES):
        _store_gdn_tile(conv_tile(c), c, gq_ref, gk_ref, gv_ref)
    bg_ref[...] = _beta_g(_dot(hb, w_ref[:, W_BG0:]), alog_ref[...], dtb_ref[...])


def _full(shape):
    return pl.BlockSpec(shape, lambda *_: (0,) * len(shape))


def _prompt_projection(x2, n_seq, seq, wn, w_proj, gsum, qn, kn, cos, sin, convw, alog, dtb):
    rows = x2.shape[0]
    tm = min(PROJ_ROWS, seq)
    assert seq % tm == 0 and tm % LANES == 0
    tps = seq // tm
    d = x2.shape[1]
    row_spec = lambda width: pl.BlockSpec((tm, width), lambda i: (i, 0))
    head_row_spec = pl.BlockSpec((A_HEADS * tm, A_HD), lambda i: (i, 0))
    pos_spec = pl.BlockSpec((tm, LANES), lambda i: (i % tps, 0))
    out_shape = (
        jax.ShapeDtypeStruct((rows, A_WIDTH), BF16),
        jax.ShapeDtypeStruct((rows * A_HEADS, A_HD), F32),
        jax.ShapeDtypeStruct((rows, A_WIDTH), BF16),
        jax.ShapeDtypeStruct((rows * A_HEADS, A_HD), F32),
        jax.ShapeDtypeStruct((n_seq, tps, A_HEADS, A_HD, tm), BF16),
        jax.ShapeDtypeStruct((rows, B_QK), F32),
        jax.ShapeDtypeStruct((rows, B_QK), F32),
        jax.ShapeDtypeStruct((rows, B_WIDTH), F32),
        jax.ShapeDtypeStruct((rows, LANES), F32),
        jax.ShapeDtypeStruct((n_seq, CONV_W - 1, B_CONV), F32),
    )
    return pl.pallas_call(
        functools.partial(_prompt_proj_kernel, tiles_per_seq=tps),
        grid=(rows // tm,),
        in_specs=[row_spec(d), _full(wn.shape), _full(w_proj.shape), _full(gsum.shape),
                  _full(qn.shape), _full(kn.shape), pos_spec, pos_spec,
                  _full(convw.shape), _full(alog.shape), _full(dtb.shape)],
        out_specs=(row_spec(A_WIDTH), head_row_spec, row_spec(A_WIDTH), head_row_spec,
                   pl.BlockSpec((1, 1, A_HEADS, A_HD, tm), lambda i: (i // tps, i % tps, 0, 0, 0)),
                   row_spec(B_QK), row_spec(B_QK), row_spec(B_WIDTH), row_spec(LANES),
                   pl.BlockSpec((1, CONV_W - 1, B_CONV), lambda i: (i // tps, 0, 0))),
        out_shape=out_shape,
        scratch_shapes=[pltpu.VMEM((tm + SUBLANES, B_CONV), F32)],
        compiler_params=pltpu.CompilerParams(dimension_semantics=("arbitrary",),
                                             vmem_limit_bytes=VMEM_LIMIT_BYTES),
        name="prompt_projection",
    )(x2, wn, w_proj, gsum, qn, kn, cos, sin, convw, alog, dtb)


def _sample_projection(x2, wn, w_proj, gsum, qn, kn, cos, sin, convw, alog, dtb, c0, c1, c2):
    rows = x2.shape[0]
    out_shape = (
        jax.ShapeDtypeStruct((rows, A_WIDTH), F32),
        jax.ShapeDtypeStruct((rows, A_WIDTH), F32),
        jax.ShapeDtypeStruct((rows, A_WIDTH), F32),
        jax.ShapeDtypeStruct((rows, B_QK), F32),
        jax.ShapeDtypeStruct((rows, B_QK), F32),
        jax.ShapeDtypeStruct((rows, B_WIDTH), F32),
        jax.ShapeDtypeStruct((rows, LANES), F32),
        jax.ShapeDtypeStruct((rows, B_CONV), F32),
    )
    args = (x2, wn, w_proj, gsum, qn, kn, cos, sin, convw, alog, dtb, c0, c1, c2)
    return pl.pallas_call(
        _sample_proj_kernel,
        grid=(1,),
        in_specs=[_full(a.shape) for a in args],
        out_specs=tuple(_full(s.shape) for s in out_shape),
        out_shape=out_shape,
        compiler_params=pltpu.CompilerParams(dimension_semantics=("arbitrary",),
                                             vmem_limit_bytes=VMEM_LIMIT_BYTES),
        name="sample_projection",
    )(*args)


def _lambda_value(lamp, lam_init):
    s1 = jnp.sum(lamp[0:1, :] * lamp[1:2, :], axis=1, keepdims=True)
    s2 = jnp.sum(lamp[2:3, :] * lamp[3:4, :], axis=1, keepdims=True)
    return jnp.exp(s1) - jnp.exp(s2) + lam_init


def _dual_attn_kernel(pt_ref, tn_ref, th_ref, tqb_ref, tkv_ref,
                      lamp_ref, subln_ref, q_ref, k_ref, vt_ref, q8_ref, k8_ref, v8_ref, *rest,
                      pages_per_step, paged_steps, chunks_per_seq, lam_init):
    del tn_ref, th_ref
    (ck_ref, cv_ref, o_ref, os_ref, m_ref, acc_ref, sm_ref, sl_ref, sacc_ref,
     kbuf_ref, vbuf_ref, ksem_ref, vsem_ref) = rest
    step = pl.program_id(0)
    qb = tqb_ref[step]
    kv = tkv_ref[step]
    tq = q_ref.shape[1]
    tk = k_ref.shape[1]
    kv_last = lax.div((qb + 1) * tq - 1, tk)
    lam = _lambda_value(lamp_ref[...], lam_init)
    subln = subln_ref[...]
    active = step < paged_steps
    n_buf = kbuf_ref.shape[0]
    slot = lax.rem(step, n_buf)

    def page_copies(page_step):
        buf = lax.rem(page_step, n_buf)
        copies = []
        for j in range(pages_per_step):
            page = pt_ref[page_step * pages_per_step + j]
            copies.append(pltpu.make_async_copy(ck_ref.at[page], kbuf_ref.at[buf, j], ksem_ref.at[buf]))
            copies.append(pltpu.make_async_copy(cv_ref.at[page], vbuf_ref.at[buf, j], vsem_ref.at[buf]))
        return copies

    @pl.when(step == 0)
    def _():
        for ahead in range(min(n_buf - 1, paged_steps)):
            for c in page_copies(ahead):
                c.start()

    @pl.when(step + (n_buf - 1) < paged_steps)
    def _():
        for c in page_copies(step + (n_buf - 1)):
            c.start()

    @pl.when(active)
    def _():
        for c in page_copies(step):
            c.wait()

    @pl.when(kv == 0)
    def _():
        m_ref[...] = jnp.full(m_ref.shape, -jnp.inf, F32)
        acc_ref[...] = jnp.zeros_like(acc_ref)

    chunk = lax.rem(step, chunks_per_seq)
    rows = 2 * A_HEADS
    page_rows = PAGE_SIZE * A_HEADS
    qrow = lax.broadcasted_iota(jnp.int32, (rows, A_HD), 0)
    qlane = lax.broadcasted_iota(jnp.int32, (rows, A_HD), 1)
    qmat = jnp.where(qlane // A_DH == qrow // A_HEADS, q8_ref[0], 0.0)

    @pl.when(jnp.logical_and(active, chunk == 0))
    def _():
        sm_ref[...] = jnp.sum(qmat * k8_ref[0], axis=1, keepdims=True)
        sl_ref[...] = jnp.ones_like(sl_ref)
        sacc_ref[...] = v8_ref[0]

    def dual_step(masked):
        q = q_ref[0]
        lane = lax.broadcasted_iota(jnp.int32, q.shape, 1)
        zero = jnp.zeros_like(q)
        kblk = k_ref[0]
        s_half = [_dot_nt(kblk, jnp.where(lane < A_DH, q, zero)),
                  _dot_nt(kblk, jnp.where(lane >= A_DH, q, zero))]
        qmb = qmat.astype(BF16)
        sc = jnp.concatenate([_dot_nt(qmb, kbuf_ref[slot, j].astype(BF16))
                              for j in range(pages_per_step)], axis=1)
        if masked:
            causal = (lax.broadcasted_iota(jnp.int32, (tk, tq), 0)
                      <= lax.broadcasted_iota(jnp.int32, (tk, tq), 1) + (qb * tq - kv * tk))
        tv = vt_ref.shape[-1]
        ones_rows = jnp.ones((acc_ref.shape[0] - A_HD, tv), BF16)
        for half in range(2):
            cols = slice(half * tq, (half + 1) * tq)
            s = jnp.where(causal, s_half[half], -jnp.inf) if masked else s_half[half]
            m_prev = m_ref[:, cols]
            m_new = jnp.maximum(m_prev, jnp.max(s, axis=0, keepdims=True))
            alpha = jnp.exp2(m_prev - m_new)
            m_ref[:, cols] = m_new
            acc = alpha * acc_ref[:, cols]
            for j in range(vt_ref.shape[1]):
                pb = jnp.exp2(s[j * tv:(j + 1) * tv] - m_new).astype(BF16)
                vt_ext = jnp.concatenate([vt_ref[0, j, 0], ones_rows], axis=0)
                acc = acc + _dot(vt_ext, pb)
            acc_ref[:, cols] = acc
        row = lax.broadcasted_iota(jnp.int32, sc.shape, 0)
        col = lax.broadcasted_iota(jnp.int32, sc.shape, 1)
        sc = jnp.where(col % A_HEADS == row % A_HEADS, sc, -jnp.inf)
        sm_prev = sm_ref[...]
        sl_prev = sl_ref[...]
        sacc_prev = sacc_ref[...]
        sm_new = jnp.maximum(sm_prev, jnp.max(sc, axis=1, keepdims=True))
        salpha = jnp.exp2(sm_prev - sm_new)
        sp = jnp.exp2(sc - sm_new)
        spb = sp.astype(BF16)
        pv = _dot(spb[:, 0:page_rows], vbuf_ref[slot, 0].astype(BF16))
        for j in range(1, pages_per_step):
            pv = pv + _dot(spb[:, j * page_rows:(j + 1) * page_rows], vbuf_ref[slot, j].astype(BF16))
        sl_ref[...] = jnp.where(active, salpha * sl_prev + jnp.sum(sp, axis=1, keepdims=True), sl_prev)
        sacc_ref[...] = jnp.where(active, salpha * sacc_prev + pv, sacc_prev)
        sm_ref[...] = jnp.where(active, sm_new, sm_prev)

    @pl.when(kv < kv_last)
    def _():
        dual_step(False)

    @pl.when(kv == kv_last)
    def _():
        dual_step(True)
        o = acc_ref[0:A_HD, :] / acc_ref[A_HD:A_HD + 1, :]
        out = (o[:, :tq] - lam * o[:, tq:]).T
        o_ref[0] = _rms_rows(out, subln) * (1.0 - lam_init)

    @pl.when(jnp.logical_and(active, chunk == chunks_per_seq - 1))
    def _():
        o = sacc_ref[...] / sl_ref[...]
        out = o[:A_HEADS] - lam * o[A_HEADS:]
        os_ref[0] = _rms_rows(out, subln) * (1.0 - lam_init)


def _dual_attention(q, kb, vt, q_s, k_s, v_s, cache_k, cache_v, page_table, lamp, subln,
                    n_seq, seq, lam_init):
    tv = vt.shape[-1]
    tq = min(ATTN_BLOCK, seq)
    tk = min(ATTN_KEY_BLOCK, seq)
    assert seq % tk == 0 and tk % tq == 0 and tq % tv == 0
    nq = seq // tq
    n_dec, n_pages = page_table.shape
    n_pool = cache_k.shape[0]
    pairs = [(n, h, i, j) for n in range(n_seq) for h in range(A_HEADS)
             for i in range(nq) for j in range(((i + 1) * tq - 1) // tk + 1)]
    steps = len(pairs)
    fits = [p for p in range(1, n_pages + 1) if n_pages % p == 0 and n_dec * (n_pages // p) <= steps]
    assert fits, "prompt attention grid too short to carry the cache page stream"
    pps = fits[0]
    cps = n_pages // pps
    paged_steps = n_dec * cps
    tabs = [jnp.asarray([p[c] for p in pairs], jnp.int32) for c in range(4)]

    q3 = q.reshape(n_seq, seq, A_WIDTH)
    k3 = kb.reshape(n_seq, seq, A_WIDTH)
    ck = cache_k.reshape(n_pool, PAGE_SIZE * A_HEADS, A_HD)
    cv = cache_v.reshape(n_pool, PAGE_SIZE * A_HEADS, A_HD)
    stack2 = lambda a: jnp.tile(a.reshape(n_dec, A_HEADS, A_HD), (1, 2, 1))
    rows = 2 * A_HEADS

    pstep = lambda s: jnp.minimum(s, paged_steps - 1)
    const = lambda shape: pl.BlockSpec(shape, lambda s, *_: (0,) * len(shape))
    q_spec = pl.BlockSpec((1, tq, A_HD), lambda s, pt, tn, th, tqb, tkv: (tn[s], tqb[s], th[s]))
    k_spec = pl.BlockSpec((1, tk, A_HD), lambda s, pt, tn, th, tqb, tkv: (tn[s], tkv[s], th[s]))
    vt_spec = pl.BlockSpec((1, tk // tv, 1, A_HD, tv),
                           lambda s, pt, tn, th, tqb, tkv: (tn[s], tkv[s], th[s], 0, 0))
    row_spec = pl.BlockSpec((1, rows, A_HD), lambda s, *_: (pstep(s) // cps, 0, 0))
    hbm_spec = pl.BlockSpec(memory_space=pl.ANY)
    page_buf = pltpu.VMEM((PAGE_RING, pps, PAGE_SIZE * A_HEADS, A_HD), F32)

    grid_spec = pltpu.PrefetchScalarGridSpec(
        num_scalar_prefetch=5,
        grid=(steps,),
        in_specs=[const(lamp.shape), const(subln.shape), q_spec, k_spec, vt_spec,
                  row_spec, row_spec, row_spec, hbm_spec, hbm_spec],
        out_specs=(q_spec, pl.BlockSpec((1, A_HEADS, A_HD), lambda s, *_: (pstep(s) // cps, 0, 0))),
        scratch_shapes=[pltpu.VMEM((1, 2 * tq), F32),
                        pltpu.VMEM((A_HD + DENOM_ROWS, 2 * tq), F32),
                        pltpu.VMEM((rows, 1), F32), pltpu.VMEM((rows, 1), F32),
                        pltpu.VMEM((rows, A_HD), F32),
                        page_buf, page_buf,
                        pltpu.SemaphoreType.DMA((PAGE_RING,)), pltpu.SemaphoreType.DMA((PAGE_RING,))],
    )
    o_p, o_s = pl.pallas_call(
        functools.partial(_dual_attn_kernel, pages_per_step=pps, paged_steps=paged_steps,
                          chunks_per_seq=cps, lam_init=lam_init),
        grid_spec=grid_spec,
        out_shape=(jax.ShapeDtypeStruct((n_seq, seq, A_WIDTH), F32),
                   jax.ShapeDtypeStruct((n_dec, A_HEADS, A_HD), F32)),
        compiler_params=pltpu.CompilerParams(dimension_semantics=("arbitrary",),
                                             vmem_limit_bytes=VMEM_LIMIT_BYTES),
        name="dual_attention",
    )(page_table.reshape(-1), *tabs, lamp, subln, q3, k3, vt, stack2(q_s), stack2(k_s), stack2(v_s), ck, cv)
    return o_p.reshape(n_seq * seq, A_WIDTH), o_s.reshape(n_dec, A_WIDTH)


def _gdn_chunk_kernel(gq_ref, gk_ref, gv_ref, bg_ref, onorm_ref, o_ref, s_out_ref, s_ref):
    tb = pl.program_id(0)
    n_seq, tt, _ = gq_ref.shape
    nc = tt // CHUNK

    @pl.when(tb == 0)
    def _():
        s_ref[...] = jnp.zeros_like(s_ref)

    ii = lax.broadcasted_iota(jnp.int32, (CHUNK, CHUNK), 0)
    jj = lax.broadcasted_iota(jnp.int32, (CHUNK, CHUNK), 1)
    incl = ii >= jj
    strict = ii > jj
    ltri = incl.astype(BF16)
    onorm = onorm_ref[...]

    heads = [(n, h) for n in range(n_seq) for h in range(B_HEADS)]

    def chunk_step(ci, carry):
        r0 = pl.multiple_of(ci * CHUNK, CHUNK)
        rows = pl.ds(r0, CHUNK)
        head_cols = lambda ref: jnp.stack([ref[n, rows, h * B_DK:(h + 1) * B_DK] for n, h in heads])
        q = head_cols(gq_ref)
        k = head_cols(gk_ref)
        v = head_cols(gv_ref)
        bgc = [bg_ref[n, rows, :] for n in range(n_seq)]
        gcum = [_dot_exact_lhs(ltri, b) for b in bgc]
        lane_bcast = lambda x, c: jnp.broadcast_to(x[:, c:c + 1], (CHUNK, B_DK))
        bh = jnp.stack([lane_bcast(bgc[n], h) for n, h in heads])
        gcs = [lane_bcast(gcum[n], B_HEADS + h) for n, h in heads]
        gc = jnp.stack(gcs)
        d = gc[:, :, :CHUNK] - jnp.stack([g.T[:CHUNK, :] for g in gcs])
        decay = jnp.exp(jnp.where(incl[None], d, -jnp.inf))
        glast = gc[:, CHUNK - 1:CHUNK, :]
        eg = jnp.exp(gc)
        kb = k * bh
        kbf = k.astype(BF16)
        m = -jnp.where(strict[None], _bmm_nt(kb.astype(BF16), kbf) * decay, 0.0)
        sol = jnp.concatenate([v * bh, kb * eg], axis=2)
        mm = lambda level, a, b: (_bmm_split(a, b) if level < SOLVE_SPLIT_LEVELS
                                  else _bmm(a.astype(BF16), b.astype(BF16)))
        sol = sol + mm(0, m, sol)
        p = m
        for level in range(5):
            p = mm(level, p, p)
            sol = sol + mm(level + 1, p, sol)
        u = sol[:, :, :B_DV]
        w = sol[:, :, B_DV:]
        qk = jnp.where(incl[None], _bmm_nt(q.astype(BF16), kbf) * decay, 0.0)
        s = s_ref[...]
        sb = s.astype(BF16)
        vn = (u - _bmm(w.astype(BF16), sb)).astype(BF16)
        o = _bmm((q * eg).astype(BF16), sb) + _bmm(qk.astype(BF16), vn)
        kg = (k * jnp.exp(glast - gc)).astype(BF16)
        s_ref[...] = s * jnp.exp(glast) + _bmm_tn(kg, vn)
        on = _rms_rows(o, onorm)
        for i, (n, h) in enumerate(heads):
            o_ref[n, rows, h * B_DV:(h + 1) * B_DV] = on[i]
        return carry

    lax.fori_loop(0, nc, chunk_step, 0)

    @pl.when(tb == pl.num_programs(0) - 1)
    def _():
        s_out_ref[...] = s_ref[...]


def _gdn_chunked(gq, gk, gv, bg, onorm, n_seq, seq):
    tt = min(GDN_ROWS, seq)
    assert seq % tt == 0 and tt % CHUNK == 0
    blk = lambda width: pl.BlockSpec((n_seq, tt, width), lambda t: (0, t, 0))
    state_shape = (n_seq * B_HEADS, B_DK, B_DV)
    out_shape = (jax.ShapeDtypeStruct((n_seq, seq, B_WIDTH), F32),
                 jax.ShapeDtypeStruct(state_shape, F32))
    o, s = pl.pallas_call(
        _gdn_chunk_kernel,
        grid=(seq // tt,),
        in_specs=[blk(B_QK), blk(B_QK), blk(B_WIDTH), blk(LANES), _full(onorm.shape)],
        out_specs=(blk(B_WIDTH), _full(state_shape)),
        out_shape=out_shape,
        scratch_shapes=[pltpu.VMEM(state_shape, F32)],
        compiler_params=pltpu.CompilerParams(dimension_semantics=("arbitrary",),
                                             vmem_limit_bytes=VMEM_LIMIT_BYTES),
        name="gdn_chunked",
    )(gq.reshape(n_seq, seq, B_QK), gk.reshape(n_seq, seq, B_QK), gv.reshape(n_seq, seq, B_WIDTH),
      bg.reshape(n_seq, seq, LANES), onorm)
    return o.reshape(n_seq * seq, B_WIDTH), s.reshape(n_seq, B_HEADS, B_DK, B_DV)


def _gdn_step_kernel(gq_ref, gk_ref, gv_ref, bg_ref, onorm_ref, s_ref, o_ref, so_ref):
    ri = lax.broadcasted_iota(jnp.int32, (B_DK, B_DK), 0)
    ci = lax.broadcasted_iota(jnp.int32, (B_DK, B_DK), 1)
    eye = (ri == ci).astype(F32)
    onorm = onorm_ref[...]
    for b in range(gq_ref.shape[0]):
        bg = bg_ref[b]
        for h in range(B_HEADS):
            sl = slice(h * B_DK, (h + 1) * B_DK)
            q = gq_ref[b][:, sl]
            k = gk_ref[b][:, sl]
            v = gv_ref[b][:, sl]
            beta = bg[:, h:h + 1]
            a = jnp.exp(bg[:, B_HEADS + h:B_HEADS + h + 1])
            kcol = jnp.sum(eye * k, axis=1, keepdims=True)
            qcol = jnp.sum(eye * q, axis=1, keepdims=True)
            s = s_ref[b, h] * a
            u = beta * (v - jnp.sum(s * kcol, axis=0, keepdims=True))
            s = s + kcol * u
            so_ref[b, h] = s
            o = jnp.sum(s * qcol, axis=0, keepdims=True)
            o_ref[b, :, sl] = _rms_rows(o, onorm)


def _gdn_step(gq, gk, gv, bg, onorm, state):
    n_dec = state.shape[0]
    bb = max(p for p in range(1, GDN_STEP_SEQS + 1) if n_dec % p == 0)
    row_spec = lambda width: pl.BlockSpec((bb, 1, width), lambda i: (i, 0, 0))
    s_spec = pl.BlockSpec((bb, B_HEADS, B_DK, B_DV), lambda i: (i, 0, 0, 0))
    out_shape = (jax.ShapeDtypeStruct((n_dec, 1, B_WIDTH), F32),
                 jax.ShapeDtypeStruct(state.shape, F32))
    o, s_new = pl.pallas_call(
        _gdn_step_kernel,
        grid=(n_dec // bb,),
        in_specs=[row_spec(B_QK), row_spec(B_QK), row_spec(B_WIDTH), row_spec(LANES),
                  pl.BlockSpec(onorm.shape, lambda i: (0, 0)), s_spec],
        out_specs=(row_spec(B_WIDTH), s_spec),
        out_shape=out_shape,
        compiler_params=pltpu.CompilerParams(dimension_semantics=("arbitrary",),
                                             vmem_limit_bytes=VMEM_LIMIT_BYTES),
        name="gdn_step",
    )(gq.reshape(n_dec, 1, B_QK), gk.reshape(n_dec, 1, B_QK), gv.reshape(n_dec, 1, B_WIDTH),
      bg.reshape(n_dec, 1, LANES), onorm, state.astype(F32))
    return o.reshape(n_dec, B_WIDTH), s_new


def _merge_out_kernel(x_ref, wn_ref, wg_ref, oa_ref, ob_ref, wua_ref, wub_ref, wo_ref, y_ref):
    x = x_ref[...]
    hb = _rms_rows(x, wn_ref[...]).astype(BF16)
    az = _dot(hb, wg_ref[:, 0:A_WIDTH])
    bz = _dot(hb, wg_ref[:, A_WIDTH:A_WIDTH + B_WIDTH])
    oa = oa_ref[...] * (az * _sigmoid(az))
    ob = ob_ref[...] * (bz * _sigmoid(bz))
    d = x.shape[1]
    g0 = A_WIDTH + B_WIDTH
    ga = _dot(hb, wg_ref[:, g0:g0 + d])
    gb = _dot(hb, wg_ref[:, g0 + d:g0 + 2 * d])
    y = (_sigmoid(ga) * _dot(oa.astype(BF16), wua_ref[...])
         + _sigmoid(gb) * _dot(ob.astype(BF16), wub_ref[...]))
    y_ref[...] = x + _dot(y.astype(BF16), wo_ref[...])


def _merge_out(x2, wn, w_gate, o_a, o_b, w_up_a, w_up_b, w_out, tm):
    rows, d = x2.shape
    assert rows % tm == 0
    row_spec = lambda width: pl.BlockSpec((tm, width), lambda i: (i, 0))
    return pl.pallas_call(
        _merge_out_kernel,
        grid=(rows // tm,),
        in_specs=[row_spec(d), _full(wn.shape), _full(w_gate.shape), row_spec(A_WIDTH), row_spec(B_WIDTH),
                  _full(w_up_a.shape), _full(w_up_b.shape), _full(w_out.shape)],
        out_specs=row_spec(d),
        out_shape=jax.ShapeDtypeStruct((rows, d), F32),
        compiler_params=pltpu.CompilerParams(dimension_semantics=("arbitrary",),
                                             vmem_limit_bytes=VMEM_LIMIT_BYTES),
        name="merge_out",
    )(x2, wn, w_gate, o_a, o_b, w_up_a, w_up_b, w_out)


def _rope_tables(pos):
    half = A_DH // 2
    inv = ROPE_THETA ** (-jnp.arange(half, dtype=F32) / half)
    ang = pos.astype(F32)[:, None] * inv[None, :]
    cos = jnp.cos(ang)
    sin = jnp.sin(ang)
    return jnp.tile(cos, (1, 4)), jnp.concatenate([-sin, sin, -sin, sin], axis=1)


def _repack_w_in(w):
    d = w.shape[0]
    o = 0
    parts = {}
    for name, size in (("aq", A_WIDTH), ("ak", A_WIDTH), ("av", A_WIDTH), ("az", A_WIDTH),
                       ("bq", B_QK), ("bk", B_QK), ("bv", B_WIDTH), ("bz", B_WIDTH),
                       ("bb", B_HEADS), ("ba", B_HEADS), ("ga", d), ("gb", d)):
        parts[name] = w[:, o:o + size]
        o += size
    assert o == w.shape[1]
    pad = jnp.zeros((d, LANES - 2 * B_HEADS), w.dtype)
    w_proj = jnp.concatenate([parts[n] for n in ("aq", "ak", "av", "bq", "bk", "bv", "bb", "ba")] + [pad], axis=1)
    w_gate = jnp.concatenate([parts[n] for n in ("az", "bz", "ga", "gb")], axis=1)
    return w_proj.astype(BF16), w_gate.astype(BF16)


def _lane_row(vec, offset):
    return jnp.zeros((1, LANES), F32).at[0, offset:offset + vec.shape[0]].set(vec.astype(F32))


def kernel(x_prompt, x_sample, cache_k, cache_v, state_conv, state_ssm, page_table, w_norm, w_in, a_qn, a_kn, a_lq1, a_lk1, a_lq2, a_lk2, a_subln, conv_w, a_log, dt_bias, b_onorm, w_up_a, w_up_b, w_out):
    depth = w_norm.shape[0]
    n_seq, seq, d = x_prompt.shape
    n_dec, dec_seq, _ = x_sample.shape
    assert dec_seq == 1, "the sample path handles one new token per sequence"
    n_pages = page_table.shape[1]
    past_len = n_pages * PAGE_SIZE

    cos_p, sin_p = _rope_tables(jnp.arange(seq, dtype=jnp.int32))
    cos_s, sin_s = _rope_tables(past_len + jnp.arange(dec_seq, dtype=jnp.int32))
    sub = lax.broadcasted_iota(jnp.int32, (A_WIDTH, A_WIDTH), 0) // A_DH
    gsum = jnp.where(sub == sub.T, 1.0 / A_DH, 0.0).astype(BF16)

    hp = x_prompt.reshape(n_seq * seq, d)
    hs = x_sample.reshape(n_dec, d)
    outs = [[] for _ in range(8)]
    for l in range(depth):
        lam_init = 0.8 - 0.6 * math.exp(-0.3 * l)
        wn = w_norm[l].reshape(1, d)
        w_proj, w_gate = _repack_w_in(w_in[l])
        qn = jnp.tile(a_qn[l], A_WIDTH // A_DH).reshape(1, A_WIDTH)
        kn = jnp.tile(a_kn[l], A_WIDTH // A_DH).reshape(1, A_WIDTH)
        lamp = jnp.stack([a_lq1[l], a_lk1[l], a_lq2[l], a_lk2[l]]).astype(F32)
        subln = a_subln[l].reshape(1, A_HD)
        alog = _lane_row(a_log[l], B_HEADS)
        dtb = _lane_row(dt_bias[l], B_HEADS)
        onorm = b_onorm[l].reshape(1, B_DV)
        wua = w_up_a[l].astype(BF16)
        wub = w_up_b[l].astype(BF16)
        wo = w_out[l].astype(BF16)
        convw = conv_w[l]

        q, k, kb, v, vt, gq, gk, gv, bg, conv_p = _prompt_projection(
            hp, n_seq, seq, wn, w_proj, gsum, qn, kn, cos_p, sin_p, convw, alog, dtb)
        sc = state_conv[l]
        qs, ks, vs, gqs, gks, gvs, bgs, us = _sample_projection(
            hs, wn, w_proj, gsum, qn, kn, cos_s, sin_s, convw, alog, dtb,
            sc[:, 0, :], sc[:, 1, :], sc[:, 2, :])
        o_a, o_as = _dual_attention(q, kb, vt, qs, ks, vs, cache_k[l], cache_v[l], page_table,
                                    lamp, subln, n_seq, seq, lam_init)
        o_b, ssm_p = _gdn_chunked(gq, gk, gv, bg, onorm, n_seq, seq)
        hp = _merge_out(hp, wn, w_gate, o_a, o_b, wua, wub, wo, min(MERGE_ROWS, seq))
        o_bs, ssm_s = _gdn_step(gqs, gks, gvs, bgs, onorm, state_ssm[l])
        hs = _merge_out(hs, wn, w_gate, o_as, o_bs, wua, wub, wo, n_dec)

        outs[0].append(k.reshape(n_seq, seq, A_HEADS, A_HD))
        outs[1].append(v.reshape(n_seq, seq, A_HEADS, A_HD))
        outs[2].append(conv_p)
        outs[3].append(ssm_p.astype(x_prompt.dtype))
        outs[4].append(ks.reshape(n_dec, dec_seq, A_HEADS, A_HD))
        outs[5].append(vs.reshape(n_dec, dec_seq, A_HEADS, A_HD))
        outs[6].append(jnp.concatenate([sc[:, 1:, :], us[:, None, :]], axis=1))
        outs[7].append(ssm_s.astype(x_sample.dtype))

    return (hp.reshape(n_seq, seq, d), hs.reshape(n_dec, dec_seq, d)) + tuple(jnp.stack(o) for o in outs)
```

```python
import functools
import math

import jax
import jax.numpy as jnp
from jax import lax
from jax.experimental import pallas as pl
from jax.experimental.pallas import tpu as pltpu

F32 = jnp.float32
BF16 = jnp.bfloat16

A_HEADS = 4
A_DH = 64
A_HD = 2 * A_DH
A_WIDTH = A_HEADS * A_HD
B_HEADS = 4
B_DK = 128
B_DV = 128
B_QK = B_HEADS * B_DK
B_WIDTH = B_HEADS * B_DV
B_CONV = 2 * B_QK + B_WIDTH
CONV_W = 4
CHUNK = 64
PAGE_SIZE = 128
ROPE_THETA = 10000.0
EPS = 1e-6

LANES = 128
SUBLANES = 8
VMEM_LIMIT_BYTES = 56 * 1024 * 1024

PROJ_ROWS = 256
ATTN_BLOCK = 512
ATTN_KEY_BLOCK = 1024
MERGE_ROWS = 256
GDN_ROWS = 256
PAGE_RING = 3
GDN_STEP_SEQS = 4
GDN_CHUNK_GROUP = 4
SOLVE_SPLIT_LEVELS = 3
DENOM_ROWS = 16
Q_SCALE = (A_DH ** -0.5) * math.log2(math.e)


def _dot(a, b):
    return jnp.dot(a, b, preferred_element_type=F32)


def _dot_nt(a, b):
    return lax.dot_general(a, b, (((1,), (1,)), ((), ())), preferred_element_type=F32)


def _bmm(a, b):
    return lax.dot_general(a, b, (((2,), (1,)), ((0,), (0,))), preferred_element_type=F32)


def _bmm_split(a, b):
    ah = a.astype(BF16)
    al = (a - ah.astype(F32)).astype(BF16)
    bh = b.astype(BF16)
    bl = (b - bh.astype(F32)).astype(BF16)
    return _bmm(jnp.concatenate([ah, al, ah], axis=2), jnp.concatenate([bh, bh, bl], axis=1))


def _bmm_nt(a, b):
    return lax.dot_general(a, b, (((2,), (2,)), ((0,), (0,))), preferred_element_type=F32)


def _bmm_tn(a, b):
    return lax.dot_general(a, b, (((1,), (1,)), ((0,), (0,))), preferred_element_type=F32)


def _split3(x):
    x1 = x.astype(BF16)
    r1 = x - x1.astype(F32)
    x2 = r1.astype(BF16)
    x3 = (r1 - x2.astype(F32)).astype(BF16)
    return x1, x2, x3


def _dot_exact_lhs(mat01, x):
    x1, x2, x3 = _split3(x)
    return _dot(mat01, x1) + _dot(mat01, x2) + _dot(mat01, x3)


def _sigmoid(x):
    return 1.0 / (1.0 + jnp.exp(-x))


def _softplus(x):
    return jnp.maximum(x, 0.0) + jnp.log1p(jnp.exp(-jnp.abs(x)))


def _rms_rows(x, gain):
    return x * lax.rsqrt(jnp.mean(x * x, axis=-1, keepdims=True) + EPS) * gain


def _sub_head_mean_sq(a, gsum_ref):
    return _dot((a * a).astype(BF16), gsum_ref[...])


def _norm_rope_heads(a, ms, gain, cos, sin):
    y = a * lax.rsqrt(ms + EPS) * gain
    lane = lax.broadcasted_iota(jnp.int32, (a.shape[0], LANES), 1)
    first_half = (lane % A_DH) < (A_DH // 2)
    heads = []
    for j in range(A_HEADS):
        t = y[:, j * A_HD:(j + 1) * A_HD]
        partner = jnp.where(first_half,
                            pltpu.roll(t, LANES - A_DH // 2, 1),
                            pltpu.roll(t, A_DH // 2, 1))
        heads.append(t * cos + partner * sin)
    return heads


def _conv_silu(taps, w, cols):
    y = taps[0] * w[0:1, cols]
    for i in range(1, CONV_W):
        y = y + taps[i] * w[i:i + 1, cols]
    return y * _sigmoid(y)


def _store_gdn_tile(t, c, gq_ref, gk_ref, gv_ref):
    group, h = divmod(c, B_HEADS)
    if group < 2:
        t = t * lax.rsqrt(jnp.sum(t * t, axis=-1, keepdims=True) + EPS)
    if group == 0:
        t = t * (B_DK ** -0.5)
    (gq_ref, gk_ref, gv_ref)[group][:, h * B_DK:(h + 1) * B_DK] = t


def _beta_g(z, alog_row, dtb_row):
    lane = lax.broadcasted_iota(jnp.int32, z.shape, 1)
    beta = _sigmoid(z)
    g = -jnp.exp(alog_row) * _softplus(z + dtb_row)
    return jnp.where(lane < B_HEADS, beta, jnp.where(lane < 2 * B_HEADS, g, 0.0))


W_Q0, W_K0, W_V0, W_U0, W_BG0 = 0, A_WIDTH, 2 * A_WIDTH, 3 * A_WIDTH, 3 * A_WIDTH + B_CONV


def _prompt_proj_kernel(x_ref, wn_ref, w_ref, gsum_ref, qn_ref, kn_ref, cos_ref, sin_ref,
                        convw_ref, alog_ref, dtb_ref,
                        q_ref, k_ref, kb_ref, v_ref, vt_ref, gq_ref, gk_ref, gv_ref, bg_ref, conv_ref,
                        ext_ref, *, tiles_per_seq):
    i = pl.program_id(0)
    tm = x_ref.shape[0]
    hb = _rms_rows(x_ref[...], wn_ref[...]).astype(BF16)
    cos = cos_ref[...]
    sin = sin_ref[...]
    w = convw_ref[...]

    @pl.when(i % tiles_per_seq == 0)
    def _():
        ext_ref[0:SUBLANES, :] = jnp.zeros((SUBLANES, B_CONV), F32)

    def conv_tiles(chunk, u_chunk):
        for half in range(2):
            c = 2 * chunk + half
            cols = slice(c * LANES, (c + 1) * LANES)
            taps = [ext_ref[SUBLANES - 3 + t:SUBLANES - 3 + t + tm, cols] for t in range(CONV_W - 1)]
            act = _conv_silu(taps + [u_chunk[:, half * LANES:(half + 1) * LANES]], w, cols)
            _store_gdn_tile(act, c, gq_ref, gk_ref, gv_ref)

    wide = 2 * LANES

    def u_matmul(chunk):
        u_chunk = _dot(hb, w_ref[:, W_U0 + chunk * wide:W_U0 + (chunk + 1) * wide])
        ext_ref[SUBLANES:SUBLANES + tm, chunk * wide:(chunk + 1) * wide] = u_chunk
        return u_chunk

    u0 = u_matmul(0)
    a_q = _dot(hb, w_ref[:, W_Q0:W_K0])
    u1 = u_matmul(1)
    conv_tiles(0, u0)
    ms_q = _sub_head_mean_sq(a_q, gsum_ref)
    u2 = u_matmul(2)
    conv_tiles(1, u1)
    a_k = _dot(hb, w_ref[:, W_K0:W_V0])
    u3 = u_matmul(3)
    conv_tiles(2, u2)
    ms_k = _sub_head_mean_sq(a_k, gsum_ref)
    u4 = u_matmul(4)
    conv_tiles(3, u3)
    v = _dot(hb, w_ref[:, W_V0:W_U0])
    u5 = u_matmul(5)
    conv_tiles(4, u4)
    z = _dot(hb, w_ref[:, W_BG0:])
    conv_tiles(5, u5)
    conv_ref[0] = ext_ref[tm + SUBLANES - (CONV_W - 1):tm + SUBLANES, :]
    ext_ref[0:SUBLANES, :] = ext_ref[tm:tm + SUBLANES, :]
    q_heads = _norm_rope_heads(a_q, ms_q, qn_ref[...], cos, sin)
    for h in range(A_HEADS):
        q_ref[:, h * A_HD:(h + 1) * A_HD] = (q_heads[h] * Q_SCALE).astype(BF16)
    k_heads = _norm_rope_heads(a_k, ms_k, kn_ref[...], cos, sin)
    for h in range(A_HEADS):
        k_ref[pl.ds(h, tm, stride=A_HEADS), :] = k_heads[h]
        kb_ref[:, h * A_HD:(h + 1) * A_HD] = k_heads[h].astype(BF16)
    for h in range(A_HEADS):
        sl = slice(h * A_HD, (h + 1) * A_HD)
        v_ref[pl.ds(h, tm, stride=A_HEADS), :] = v[:, sl]
        vt_ref[0, 0, h] = v[:, sl].T.astype(BF16)
    bg_ref[...] = _beta_g(z, alog_ref[...], dtb_ref[...])


def _sample_proj_kernel(x_ref, wn_ref, w_ref, gsum_ref, qn_ref, kn_ref, cos_ref, sin_ref,
                        convw_ref, alog_ref, dtb_ref, c0_ref, c1_ref, c2_ref,
                        q_ref, k_ref, v_ref, gq_ref, gk_ref, gv_ref, bg_ref, u_ref):
    hb = _rms_rows(x_ref[...], wn_ref[...]).astype(BF16)
    cos = cos_ref[...]
    sin = sin_ref[...]
    a_q = _dot(hb, w_ref[:, W_Q0:W_K0])
    a_k = _dot(hb, w_ref[:, W_K0:W_V0])
    q_heads = _norm_rope_heads(a_q, _sub_head_mean_sq(a_q, gsum_ref), qn_ref[...], cos, sin)
    k_heads = _norm_rope_heads(a_k, _sub_head_mean_sq(a_k, gsum_ref), kn_ref[...], cos, sin)
    for h in range(A_HEADS):
        sl = slice(h * A_HD, (h + 1) * A_HD)
        q_ref[:, sl] = q_heads[h] * Q_SCALE
        k_ref[:, sl] = k_heads[h]
    v_ref[...] = _dot(hb, w_ref[:, W_V0:W_U0])
    u = _dot(hb, w_ref[:, W_U0:W_BG0])
    u_ref[...] = u
    w = convw_ref[...]

    def conv_tile(c):
        cols = slice(c * LANES, (c + 1) * LANES)
        return _conv_silu([c0_ref[:, cols], c1_ref[:, cols], c2_ref[:, cols], u[:, cols]], w, cols)

    for c in range(B_CONV // LANES):
        _store_gdn_tile(conv_tile(c), c, gq_ref, gk_ref, gv_ref)
    bg_ref[...] = _beta_g(_dot(hb, w_ref[:, W_BG0:]), alog_ref[...], dtb_ref[...])


def _full(shape):
    return pl.BlockSpec(shape, lambda *_: (0,) * len(shape))


def _prompt_projection(x2, n_seq, seq, wn, w_proj, gsum, qn, kn, cos, sin, convw, alog, dtb):
    rows = x2.shape[0]
    tm = min(PROJ_ROWS, seq)
    assert seq % tm == 0 and tm % LANES == 0
    tps = seq // tm
    d = x2.shape[1]
    row_spec = lambda width: pl.BlockSpec((tm, width), lambda i: (i, 0))
    head_row_spec = pl.BlockSpec((A_HEADS * tm, A_HD), lambda i: (i, 0))
    pos_spec = pl.BlockSpec((tm, LANES), lambda i: (i % tps, 0))
    out_shape = (
        jax.ShapeDtypeStruct((rows, A_WIDTH), BF16),
        jax.ShapeDtypeStruct((rows * A_HEADS, A_HD), F32),
        jax.ShapeDtypeStruct((rows, A_WIDTH), BF16),
        jax.ShapeDtypeStruct((rows * A_HEADS, A_HD), F32),
        jax.ShapeDtypeStruct((n_seq, tps, A_HEADS, A_HD, tm), BF16),
        jax.ShapeDtypeStruct((rows, B_QK), F32),
        jax.ShapeDtypeStruct((rows, B_QK), F32),
        jax.ShapeDtypeStruct((rows, B_WIDTH), F32),
        jax.ShapeDtypeStruct((rows, LANES), F32),
        jax.ShapeDtypeStruct((n_seq, CONV_W - 1, B_CONV), F32),
    )
    return pl.pallas_call(
        functools.partial(_prompt_proj_kernel, tiles_per_seq=tps),
        grid=(rows // tm,),
        in_specs=[row_spec(d), _full(wn.shape), _full(w_proj.shape), _full(gsum.shape),
                  _full(qn.shape), _full(kn.shape), pos_spec, pos_spec,
                  _full(convw.shape), _full(alog.shape), _full(dtb.shape)],
        out_specs=(row_spec(A_WIDTH), head_row_spec, row_spec(A_WIDTH), head_row_spec,
                   pl.BlockSpec((1, 1, A_HEADS, A_HD, tm), lambda i: (i // tps, i % tps, 0, 0, 0)),
                   row_spec(B_QK), row_spec(B_QK), row_spec(B_WIDTH), row_spec(LANES),
                   pl.BlockSpec((1, CONV_W - 1, B_CONV), lambda i: (i // tps, 0, 0))),
        out_shape=out_shape,
        scratch_shapes=[pltpu.VMEM((tm + SUBLANES, B_CONV), F32)],
        compiler_params=pltpu.CompilerParams(dimension_semantics=("arbitrary",),
                                             vmem_limit_bytes=VMEM_LIMIT_BYTES),
        name="prompt_projection",
    )(x2, wn, w_proj, gsum, qn, kn, cos, sin, convw, alog, dtb)


def _sample_projection(x2, wn, w_proj, gsum, qn, kn, cos, sin, convw, alog, dtb, c0, c1, c2):
    rows = x2.shape[0]
    out_shape = (
        jax.ShapeDtypeStruct((rows, A_WIDTH), F32),
        jax.ShapeDtypeStruct((rows, A_WIDTH), F32),
        jax.ShapeDtypeStruct((rows, A_WIDTH), F32),
        jax.ShapeDtypeStruct((rows, B_QK), F32),
        jax.ShapeDtypeStruct((rows, B_QK), F32),
        jax.ShapeDtypeStruct((rows, B_WIDTH), F32),
        jax.ShapeDtypeStruct((rows, LANES), F32),
        jax.ShapeDtypeStruct((rows, B_CONV), F32),
    )
    args = (x2, wn, w_proj, gsum, qn, kn, cos, sin, convw, alog, dtb, c0, c1, c2)
    return pl.pallas_call(
        _sample_proj_kernel,
        grid=(1,),
        in_specs=[_full(a.shape) for a in args],
        out_specs=tuple(_full(s.shape) for s in out_shape),
        out_shape=out_shape,
        compiler_params=pltpu.CompilerParams(dimension_semantics=("arbitrary",),
                                             vmem_limit_bytes=VMEM_LIMIT_BYTES),
        name="sample_projection",
    )(*args)


def _lambda_value(lamp, lam_init):
    s1 = jnp.sum(lamp[0:1, :] * lamp[1:2, :], axis=1, keepdims=True)
    s2 = jnp.sum(lamp[2:3, :] * lamp[3:4, :], axis=1, keepdims=True)
    return jnp.exp(s1) - jnp.exp(s2) + lam_init


def _dual_attn_kernel(pt_ref, tn_ref, th_ref, tqb_ref, tkv_ref,
                      lamp_ref, subln_ref, q_ref, k_ref, vt_ref, q8_ref, k8_ref, v8_ref, *rest,
                      pages_per_step, paged_steps, chunks_per_seq, lam_init):
    del tn_ref, th_ref
    (ck_ref, cv_ref, o_ref, os_ref, m_ref, acc_ref, sm_ref, sl_ref, sacc_ref,
     kbuf_ref, vbuf_ref, ksem_ref, vsem_ref) = rest
    step = pl.program_id(0)
    qb = tqb_ref[step]
    kv = tkv_ref[step]
    tq = q_ref.shape[1]
    tk = k_ref.shape[1]
    kv_last = lax.div((qb + 1) * tq - 1, tk)
    lam = _lambda_value(lamp_ref[...], lam_init)
    subln = subln_ref[...]
    active = step < paged_steps
    n_buf = kbuf_ref.shape[0]
    slot = lax.rem(step, n_buf)

    def page_copies(page_step):
        buf = lax.rem(page_step, n_buf)
        copies = []
        for j in range(pages_per_step):
            page = pt_ref[page_step * pages_per_step + j]
            copies.append(pltpu.make_async_copy(ck_ref.at[page], kbuf_ref.at[buf, j], ksem_ref.at[buf]))
            copies.append(pltpu.make_async_copy(cv_ref.at[page], vbuf_ref.at[buf, j], vsem_ref.at[buf]))
        return copies

    @pl.when(step == 0)
    def _():
        for ahead in range(min(n_buf - 1, paged_steps)):
            for c in page_copies(ahead):
                c.start()

    @pl.when(step + (n_buf - 1) < paged_steps)
    def _():
        for c in page_copies(step + (n_buf - 1)):
            c.start()

    @pl.when(active)
    def _():
        for c in page_copies(step):
            c.wait()

    @pl.when(kv == 0)
    def _():
        m_ref[...] = jnp.full(m_ref.shape, -jnp.inf, F32)
        acc_ref[...] = jnp.zeros_like(acc_ref)

    chunk = lax.rem(step, chunks_per_seq)
    rows = 2 * A_HEADS
    page_rows = PAGE_SIZE * A_HEADS
    qrow = lax.broadcasted_iota(jnp.int32, (rows, A_HD), 0)
    qlane = lax.broadcasted_iota(jnp.int32, (rows, A_HD), 1)
    qmat = jnp.where(qlane // A_DH == qrow // A_HEADS, q8_ref[0], 0.0)

    @pl.when(jnp.logical_and(active, chunk == 0))
    def _():
        sm_ref[...] = jnp.sum(qmat * k8_ref[0], axis=1, keepdims=True)
        sl_ref[...] = jnp.ones_like(sl_ref)
        sacc_ref[...] = v8_ref[0]

    def dual_step(masked):
        q = q_ref[0]
        lane = lax.broadcasted_iota(jnp.int32, q.shape, 1)
        zero = jnp.zeros_like(q)
        kblk = k_ref[0]
        s_half = [_dot_nt(kblk, jnp.where(lane < A_DH, q, zero)),
                  _dot_nt(kblk, jnp.where(lane >= A_DH, q, zero))]
        qmb = qmat.astype(BF16)
        sc = jnp.concatenate([_dot_nt(qmb, kbuf_ref[slot, j].astype(BF16))
                              for j in range(pages_per_step)], axis=1)
        if masked:
            causal = (lax.broadcasted_iota(jnp.int32, (tk, tq), 0)
                      <= lax.broadcasted_iota(jnp.int32, (tk, tq), 1) + (qb * tq - kv * tk))
        tv = vt_ref.shape[-1]
        ones_rows = jnp.ones((acc_ref.shape[0] - A_HD, tv), BF16)
        for half in range(2):
            cols = slice(half * tq, (half + 1) * tq)
            s = jnp.where(causal, s_half[half], -jnp.inf) if masked else s_half[half]
            m_prev = m_ref[:, cols]
            m_new = jnp.maximum(m_prev, jnp.max(s, axis=0, keepdims=True))
            alpha = jnp.exp2(m_prev - m_new)
            m_ref[:, cols] = m_new
            acc = alpha * acc_ref[:, cols]
            for j in range(vt_ref.shape[1]):
                pb = jnp.exp2(s[j * tv:(j + 1) * tv] - m_new).astype(BF16)
                vt_ext = jnp.concatenate([vt_ref[0, j, 0], ones_rows], axis=0)
                acc = acc + _dot(vt_ext, pb)
            acc_ref[:, cols] = acc
        row = lax.broadcasted_iota(jnp.int32, sc.shape, 0)
        col = lax.broadcasted_iota(jnp.int32, sc.shape, 1)
        sc = jnp.where(col % A_HEADS == row % A_HEADS, sc, -jnp.inf)
        sm_prev = sm_ref[...]
        sl_prev = sl_ref[...]
        sacc_prev = sacc_ref[...]
        sm_new = jnp.maximum(sm_prev, jnp.max(sc, axis=1, keepdims=True))
        salpha = jnp.exp2(sm_prev - sm_new)
        sp = jnp.exp2(sc - sm_new)
        spb = sp.astype(BF16)
        pv = _dot(spb[:, 0:page_rows], vbuf_ref[slot, 0].astype(BF16))
        for j in range(1, pages_per_step):
            pv = pv + _dot(spb[:, j * page_rows:(j + 1) * page_rows], vbuf_ref[slot, j].astype(BF16))
        sl_ref[...] = jnp.where(active, salpha * sl_prev + jnp.sum(sp, axis=1, keepdims=True), sl_prev)
        sacc_ref[...] = jnp.where(active, salpha * sacc_prev + pv, sacc_prev)
        sm_ref[...] = jnp.where(active, sm_new, sm_prev)

    @pl.when(kv < kv_last)
    def _():
        dual_step(False)

    @pl.when(kv == kv_last)
    def _():
        dual_step(True)
        o = acc_ref[0:A_HD, :] / acc_ref[A_HD:A_HD + 1, :]
        out = (o[:, :tq] - lam * o[:, tq:]).T
        o_ref[0] = _rms_rows(out, subln) * (1.0 - lam_init)

    @pl.when(jnp.logical_and(active, chunk == chunks_per_seq - 1))
    def _():
        o = sacc_ref[...] / sl_ref[...]
        out = o[:A_HEADS] - lam * o[A_HEADS:]
        os_ref[0] = _rms_rows(out, subln) * (1.0 - lam_init)


def _dual_attention(q, kb, vt, q_s, k_s, v_s, cache_k, cache_v, page_table, lamp, subln,
                    n_seq, seq, lam_init):
    tv = vt.shape[-1]
    tq = min(ATTN_BLOCK, seq)
    tk = min(ATTN_KEY_BLOCK, seq)
    assert seq % tk == 0 and tk % tq == 0 and tq % tv == 0
    nq = seq // tq
    n_dec, n_pages = page_table.shape
    n_pool = cache_k.shape[0]
    pairs = [(n, h, i, j) for n in range(n_seq) for h in range(A_HEADS)
             for i in range(nq) for j in range(((i + 1) * tq - 1) // tk + 1)]
    steps = len(pairs)
    fits = [p for p in range(1, n_pages + 1) if n_pages % p == 0 and n_dec * (n_pages // p) <= steps]
    assert fits, "prompt attention grid too short to carry the cache page stream"
    pps = fits[0]
    cps = n_pages // pps
    paged_steps = n_dec * cps
    tabs = [jnp.asarray([p[c] for p in pairs], jnp.int32) for c in range(4)]

    q3 = q.reshape(n_seq, seq, A_WIDTH)
    k3 = kb.reshape(n_seq, seq, A_WIDTH)
    ck = cache_k.reshape(n_pool, PAGE_SIZE * A_HEADS, A_HD)
    cv = cache_v.reshape(n_pool, PAGE_SIZE * A_HEADS, A_HD)
    stack2 = lambda a: jnp.tile(a.reshape(n_dec, A_HEADS, A_HD), (1, 2, 1))
    rows = 2 * A_HEADS

    pstep = lambda s: jnp.minimum(s, paged_steps - 1)
    const = lambda shape: pl.BlockSpec(shape, lambda s, *_: (0,) * len(shape))
    q_spec = pl.BlockSpec((1, tq, A_HD), lambda s, pt, tn, th, tqb, tkv: (tn[s], tqb[s], th[s]))
    k_spec = pl.BlockSpec((1, tk, A_HD), lambda s, pt, tn, th, tqb, tkv: (tn[s], tkv[s], th[s]))
    vt_spec = pl.BlockSpec((1, tk // tv, 1, A_HD, tv),
                           lambda s, pt, tn, th, tqb, tkv: (tn[s], tkv[s], th[s], 0, 0))
    row_spec = pl.BlockSpec((1, rows, A_HD), lambda s, *_: (pstep(s) // cps, 0, 0))
    hbm_spec = pl.BlockSpec(memory_space=pl.ANY)
    page_buf = pltpu.VMEM((PAGE_RING, pps, PAGE_SIZE * A_HEADS, A_HD), F32)

    grid_spec = pltpu.PrefetchScalarGridSpec(
        num_scalar_prefetch=5,
        grid=(steps,),
        in_specs=[const(lamp.shape), const(subln.shape), q_spec, k_spec, vt_spec,
                  row_spec, row_spec, row_spec, hbm_spec, hbm_spec],
        out_specs=(q_spec, pl.BlockSpec((1, A_HEADS, A_HD), lambda s, *_: (pstep(s) // cps, 0, 0))),
        scratch_shapes=[pltpu.VMEM((1, 2 * tq), F32),
                        pltpu.VMEM((A_HD + DENOM_ROWS, 2 * tq), F32),
                        pltpu.VMEM((rows, 1), F32), pltpu.VMEM((rows, 1), F32),
                        pltpu.VMEM((rows, A_HD), F32),
                        page_buf, page_buf,
                        pltpu.SemaphoreType.DMA((PAGE_RING,)), pltpu.SemaphoreType.DMA((PAGE_RING,))],
    )
    o_p, o_s = pl.pallas_call(
        functools.partial(_dual_attn_kernel, pages_per_step=pps, paged_steps=paged_steps,
                          chunks_per_seq=cps, lam_init=lam_init),
        grid_spec=grid_spec,
        out_shape=(jax.ShapeDtypeStruct((n_seq, seq, A_WIDTH), F32),
                   jax.ShapeDtypeStruct((n_dec, A_HEADS, A_HD), F32)),
        compiler_params=pltpu.CompilerParams(dimension_semantics=("arbitrary",),
                                             vmem_limit_bytes=VMEM_LIMIT_BYTES),
        name="dual_attention",
    )(page_table.reshape(-1), *tabs, lamp, subln, q3, k3, vt, stack2(q_s), stack2(k_s), stack2(v_s), ck, cv)
    return o_p.reshape(n_seq * seq, A_WIDTH), o_s.reshape(n_dec, A_WIDTH)


def _gdn_chunk_kernel(gq_ref, gk_ref, gv_ref, bg_ref, onorm_ref, o_ref, s_out_ref, s_ref):
    tb = pl.program_id(0)
    n_seq, tt, _ = gq_ref.shape
    nc = tt // CHUNK

    @pl.when(tb == 0)
    def _():
        s_ref[...] = jnp.zeros_like(s_ref)

    ii = lax.broadcasted_iota(jnp.int32, (CHUNK, CHUNK), 0)
    jj = lax.broadcasted_iota(jnp.int32, (CHUNK, CHUNK), 1)
    incl = ii >= jj
    strict = ii > jj
    ltri = incl.astype(BF16)
    onorm = onorm_ref[...]

    heads = [(n, h) for n in range(n_seq) for h in range(B_HEADS)]
    n_heads = len(heads)
    group = GDN_CHUNK_GROUP if nc % GDN_CHUNK_GROUP == 0 else 1
    chains = [(cc, n, h) for cc in range(group) for n, h in heads]

    def chunk_group_step(gi, carry):
        rows = [pl.ds(pl.multiple_of((gi * group + cc) * CHUNK, CHUNK), CHUNK) for cc in range(group)]
        head_cols = lambda ref: jnp.stack([ref[n, rows[cc], h * B_DK:(h + 1) * B_DK] for cc, n, h in chains])
        q = head_cols(gq_ref)
        k = head_cols(gk_ref)
        v = head_cols(gv_ref)
        bgc = {(cc, n): bg_ref[n, rows[cc], :] for cc in range(group) for n in range(n_seq)}
        gcum = {key: _dot_exact_lhs(ltri, b) for key, b in bgc.items()}
        lane_bcast = lambda x, c: jnp.broadcast_to(x[:, c:c + 1], (CHUNK, B_DK))
        bh = jnp.stack([lane_bcast(bgc[cc, n], h) for cc, n, h in chains])
        gcs = [lane_bcast(gcum[cc, n], B_HEADS + h) for cc, n, h in chains]
        gc = jnp.stack(gcs)
        d = gc[:, :, :CHUNK] - jnp.stack([g.T[:CHUNK, :] for g in gcs])
        decay = jnp.exp(jnp.where(incl[None], d, -jnp.inf))
        glast = gc[:, CHUNK - 1:CHUNK, :]
        eg = jnp.exp(gc)
        kb = k * bh
        kbf = k.astype(BF16)
        m = -jnp.where(strict[None], _bmm_nt(kb.astype(BF16), kbf) * decay, 0.0)
        sol = jnp.concatenate([v * bh, kb * eg], axis=2)
        mm = lambda level, a, b: (_bmm_split(a, b) if level < SOLVE_SPLIT_LEVELS
                                  else _bmm(a.astype(BF16), b.astype(BF16)))
        sol = sol + mm(0, m, sol)
        p = m
        for level in range(5):
            p = mm(level, p, p)
            sol = sol + mm(level + 1, p, sol)
        u = sol[:, :, :B_DV]
        wb = sol[:, :, B_DV:].astype(BF16)
        qk = jnp.where(incl[None], _bmm_nt(q.astype(BF16), kbf) * decay, 0.0).astype(BF16)
        qg = (q * eg).astype(BF16)
        kg = (k * jnp.exp(glast - gc)).astype(BF16)
        gl = jnp.exp(glast)
        for cc in range(group):
            part = slice(cc * n_heads, (cc + 1) * n_heads)
            s = s_ref[...]
            sb = s.astype(BF16)
            vn = (u[part] - _bmm(wb[part], sb)).astype(BF16)
            o = _bmm(qg[part], sb) + _bmm(qk[part], vn)
            s_ref[...] = s * gl[part] + _bmm_tn(kg[part], vn)
            on = _rms_rows(o, onorm)
            for i, (n, h) in enumerate(heads):
                o_ref[n, rows[cc], h * B_DV:(h + 1) * B_DV] = on[i]
        return carry

    lax.fori_loop(0, nc // group, chunk_group_step, 0)

    @pl.when(tb == pl.num_programs(0) - 1)
    def _():
        s_out_ref[...] = s_ref[...]


def _gdn_chunked(gq, gk, gv, bg, onorm, n_seq, seq):
    tt = min(GDN_ROWS, seq)
    assert seq % tt == 0 and tt % CHUNK == 0
    blk = lambda width: pl.BlockSpec((n_seq, tt, width), lambda t: (0, t, 0))
    state_shape = (n_seq * B_HEADS, B_DK, B_DV)
    out_shape = (jax.ShapeDtypeStruct((n_seq, seq, B_WIDTH), F32),
                 jax.ShapeDtypeStruct(state_shape, F32))
    o, s = pl.pallas_call(
        _gdn_chunk_kernel,
        grid=(seq // tt,),
        in_specs=[blk(B_QK), blk(B_QK), blk(B_WIDTH), blk(LANES), _full(onorm.shape)],
        out_specs=(blk(B_WIDTH), _full(state_shape)),
        out_shape=out_shape,
        scratch_shapes=[pltpu.VMEM(state_shape, F32)],
        compiler_params=pltpu.CompilerParams(dimension_semantics=("arbitrary",),
                                             vmem_limit_bytes=VMEM_LIMIT_BYTES),
        name="gdn_chunked",
    )(gq.reshape(n_seq, seq, B_QK), gk.reshape(n_seq, seq, B_QK), gv.reshape(n_seq, seq, B_WIDTH),
      bg.reshape(n_seq, seq, LANES), onorm)
    return o.reshape(n_seq * seq, B_WIDTH), s.reshape(n_seq, B_HEADS, B_DK, B_DV)


def _gdn_step_kernel(gq_ref, gk_ref, gv_ref, bg_ref, onorm_ref, s_ref, o_ref, so_ref):
    ri = lax.broadcasted_iota(jnp.int32, (B_DK, B_DK), 0)
    ci = lax.broadcasted_iota(jnp.int32, (B_DK, B_DK), 1)
    eye = (ri == ci).astype(F32)
    onorm = onorm_ref[...]
    for b in range(gq_ref.shape[0]):
        bg = bg_ref[b]
        for h in range(B_HEADS):
            sl = slice(h * B_DK, (h + 1) * B_DK)
            q = gq_ref[b][:, sl]
            k = gk_ref[b][:, sl]
            v = gv_ref[b][:, sl]
            beta = bg[:, h:h + 1]
            a = jnp.exp(bg[:, B_HEADS + h:B_HEADS + h + 1])
            kcol = jnp.sum(eye * k, axis=1, keepdims=True)
            qcol = jnp.sum(eye * q, axis=1, keepdims=True)
            s = s_ref[b, h] * a
            u = beta * (v - jnp.sum(s * kcol, axis=0, keepdims=True))
            s = s + kcol * u
            so_ref[b, h] = s
            o = jnp.sum(s * qcol, axis=0, keepdims=True)
            o_ref[b, :, sl] = _rms_rows(o, onorm)


def _gdn_step(gq, gk, gv, bg, onorm, state):
    n_dec = state.shape[0]
    bb = max(p for p in range(1, GDN_STEP_SEQS + 1) if n_dec % p == 0)
    row_spec = lambda width: pl.BlockSpec((bb, 1, width), lambda i: (i, 0, 0))
    s_spec = pl.BlockSpec((bb, B_HEADS, B_DK, B_DV), lambda i: (i, 0, 0, 0))
    out_shape = (jax.ShapeDtypeStruct((n_dec, 1, B_WIDTH), F32),
                 jax.ShapeDtypeStruct(state.shape, F32))
    o, s_new = pl.pallas_call(
        _gdn_step_kernel,
        grid=(n_dec // bb,),
        in_specs=[row_spec(B_QK), row_spec(B_QK), row_spec(B_WIDTH), row_spec(LANES),
                  pl.BlockSpec(onorm.shape, lambda i: (0, 0)), s_spec],
        out_specs=(row_spec(B_WIDTH), s_spec),
        out_shape=out_shape,
        compiler_params=pltpu.CompilerParams(dimension_semantics=("arbitrary",),
                                             vmem_limit_bytes=VMEM_LIMIT_BYTES),
        name="gdn_step",
    )(gq.reshape(n_dec, 1, B_QK), gk.reshape(n_dec, 1, B_QK), gv.reshape(n_dec, 1, B_WIDTH),
      bg.reshape(n_dec, 1, LANES), onorm, state.astype(F32))
    return o.reshape(n_dec, B_WIDTH), s_new


def _merge_out_kernel(x_ref, wn_ref, wg_ref, oa_ref, ob_ref, wua_ref, wub_ref, wo_ref, y_ref):
    x = x_ref[...]
    hb = _rms_rows(x, wn_ref[...]).astype(BF16)
    az = _dot(hb, wg_ref[:, 0:A_WIDTH])
    bz = _dot(hb, wg_ref[:, A_WIDTH:A_WIDTH + B_WIDTH])
    oa = oa_ref[...] * (az * _sigmoid(az))
    ob = ob_ref[...] * (bz * _sigmoid(bz))
    d = x.shape[1]
    g0 = A_WIDTH + B_WIDTH
    ga = _dot(hb, wg_ref[:, g0:g0 + d])
    gb = _dot(hb, wg_ref[:, g0 + d:g0 + 2 * d])
    y = (_sigmoid(ga) * _dot(oa.astype(BF16), wua_ref[...])
         + _sigmoid(gb) * _dot(ob.astype(BF16), wub_ref[...]))
    y_ref[...] = x + _dot(y.astype(BF16), wo_ref[...])


def _merge_out(x2, wn, w_gate, o_a, o_b, w_up_a, w_up_b, w_out, tm):
    rows, d = x2.shape
    assert rows % tm == 0
    row_spec = lambda width: pl.BlockSpec((tm, width), lambda i: (i, 0))
    return pl.pallas_call(
        _merge_out_kernel,
        grid=(rows // tm,),
        in_specs=[row_spec(d), _full(wn.shape), _full(w_gate.shape), row_spec(A_WIDTH), row_spec(B_WIDTH),
                  _full(w_up_a.shape), _full(w_up_b.shape), _full(w_out.shape)],
        out_specs=row_spec(d),
        out_shape=jax.ShapeDtypeStruct((rows, d), F32),
        compiler_params=pltpu.CompilerParams(dimension_semantics=("arbitrary",),
                                             vmem_limit_bytes=VMEM_LIMIT_BYTES),
        name="merge_out",
    )(x2, wn, w_gate, o_a, o_b, w_up_a, w_up_b, w_out)


def _rope_tables(pos):
    half = A_DH // 2
    inv = ROPE_THETA ** (-jnp.arange(half, dtype=F32) / half)
    ang = pos.astype(F32)[:, None] * inv[None, :]
    cos = jnp.cos(ang)
    sin = jnp.sin(ang)
    return jnp.tile(cos, (1, 4)), jnp.concatenate([-sin, sin, -sin, sin], axis=1)


def _repack_w_in(w):
    d = w.shape[0]
    o = 0
    parts = {}
    for name, size in (("aq", A_WIDTH), ("ak", A_WIDTH), ("av", A_WIDTH), ("az", A_WIDTH),
                       ("bq", B_QK), ("bk", B_QK), ("bv", B_WIDTH), ("bz", B_WIDTH),
                       ("bb", B_HEADS), ("ba", B_HEADS), ("ga", d), ("gb", d)):
        parts[name] = w[:, o:o + size]
        o += size
    assert o == w.shape[1]
    pad = jnp.zeros((d, LANES - 2 * B_HEADS), w.dtype)
    w_proj = jnp.concatenate([parts[n] for n in ("aq", "ak", "av", "bq", "bk", "bv", "bb", "ba")] + [pad], axis=1)
    w_gate = jnp.concatenate([parts[n] for n in ("az", "bz", "ga", "gb")], axis=1)
    return w_proj.astype(BF16), w_gate.astype(BF16)


def _lane_row(vec, offset):
    return jnp.zeros((1, LANES), F32).at[0, offset:offset + vec.shape[0]].set(vec.astype(F32))


def kernel(x_prompt, x_sample, cache_k, cache_v, state_conv, state_ssm, page_table, w_norm, w_in, a_qn, a_kn, a_lq1, a_lk1, a_lq2, a_lk2, a_subln, conv_w, a_log, dt_bias, b_onorm, w_up_a, w_up_b, w_out):
    depth = w_norm.shape[0]
    n_seq, seq, d = x_prompt.shape
    n_dec, dec_seq, _ = x_sample.shape
    assert dec_seq == 1, "the sample path handles one new token per sequence"
    n_pages = page_table.shape[1]
    past_len = n_pages * PAGE_SIZE

    cos_p, sin_p = _rope_tables(jnp.arange(seq, dtype=jnp.int32))
    cos_s, sin_s = _rope_tables(past_len + jnp.arange(dec_seq, dtype=jnp.int32))
    sub = lax.broadcasted_iota(jnp.int32, (A_WIDTH, A_WIDTH), 0) // A_DH
    gsum = jnp.where(sub == sub.T, 1.0 / A_DH, 0.0).astype(BF16)

    hp = x_prompt.reshape(n_seq * seq, d)
    hs = x_sample.reshape(n_dec, d)
    outs = [[] for _ in range(8)]
    for l in range(depth):
        lam_init = 0.8 - 0.6 * math.exp(-0.3 * l)
        wn = w_norm[l].reshape(1, d)
        w_proj, w_gate = _repack_w_in(w_in[l])
        qn = jnp.tile(a_qn[l], A_WIDTH // A_DH).reshape(1, A_WIDTH)
        kn = jnp.tile(a_kn[l], A_WIDTH // A_DH).reshape(1, A_WIDTH)
        lamp = jnp.stack([a_lq1[l], a_lk1[l], a_lq2[l], a_lk2[l]]).astype(F32)
        subln = a_subln[l].reshape(1, A_HD)
        alog = _lane_row(a_log[l], B_HEADS)
        dtb = _lane_row(dt_bias[l], B_HEADS)
        onorm = b_onorm[l].reshape(1, B_DV)
        wua = w_up_a[l].astype(BF16)
        wub = w_up_b[l].astype(BF16)
        wo = w_out[l].astype(BF16)
        convw = conv_w[l]

        q, k, kb, v, vt, gq, gk, gv, bg, conv_p = _prompt_projection(
            hp, n_seq, seq, wn, w_proj, gsum, qn, kn, cos_p, sin_p, convw, alog, dtb)
        sc = state_conv[l]
        qs, ks, vs, gqs, gks, gvs, bgs, us = _sample_projection(
            hs, wn, w_proj, gsum, qn, kn, cos_s, sin_s, convw, alog, dtb,
            sc[:, 0, :], sc[:, 1, :], sc[:, 2, :])
        o_a, o_as = _dual_attention(q, kb, vt, qs, ks, vs, cache_k[l], cache_v[l], page_table,
                                    lamp, subln, n_seq, seq, lam_init)
        o_b, ssm_p = _gdn_chunked(gq, gk, gv, bg, onorm, n_seq, seq)
        hp = _merge_out(hp, wn, w_gate, o_a, o_b, wua, wub, wo, min(MERGE_ROWS, seq))
        o_bs, ssm_s = _gdn_step(gqs, gks, gvs, bgs, onorm, state_ssm[l])
        hs = _merge_out(hs, wn, w_gate, o_as, o_bs, wua, wub, wo, n_dec)

        outs[0].append(k.reshape(n_seq, seq, A_HEADS, A_HD))
        outs[1].append(v.reshape(n_seq, seq, A_HEADS, A_HD))
        outs[2].append(conv_p)
        outs[3].append(ssm_p.astype(x_prompt.dtype))
        outs[4].append(ks.reshape(n_dec, dec_seq, A_HEADS, A_HD))
        outs[5].append(vs.reshape(n_dec, dec_seq, A_HEADS, A_HD))
        outs[6].append(jnp.concatenate([sc[:, 1:, :], us[:, None, :]], axis=1))
        outs[7].append(ssm_s.astype(x_sample.dtype))

    return (hp.reshape(n_seq, seq, d), hs.reshape(n_dec, dec_seq, d)) + tuple(jnp.stack(o) for o in outs)
```

```python
import functools
import math

import jax
import jax.numpy as jnp
from jax import lax
from jax.experimental import pallas as pl
from jax.experimental.pallas import tpu as pltpu

F32 = jnp.float32
BF16 = jnp.bfloat16

A_HEADS = 4
A_DH = 64
A_HD = 2 * A_DH
A_WIDTH = A_HEADS * A_HD
B_HEADS = 4
B_DK = 128
B_DV = 128
B_QK = B_HEADS * B_DK
B_WIDTH = B_HEADS * B_DV
B_CONV = 2 * B_QK + B_WIDTH
CONV_W = 4
CHUNK = 64
PAGE_SIZE = 128
ROPE_THETA = 10000.0
EPS = 1e-6

LANES = 128
SUBLANES = 8
VMEM_LIMIT_BYTES = 56 * 1024 * 1024

PROJ_ROWS = 256
ATTN_BLOCK = 1024
ATTN_KEY_BLOCK = 1024
MERGE_ROWS = 256
GDN_ROWS = 256
PAGE_RING = 2
GDN_STEP_SEQS = 4
GDN_CHUNK_GROUP = 4
SOLVE_SPLIT_LEVELS = 3
DENOM_ROWS = 16
Q_SCALE = (A_DH ** -0.5) * math.log2(math.e)


def _dot(a, b):
    return jnp.dot(a, b, preferred_element_type=F32)


def _dot_nt(a, b):
    return lax.dot_general(a, b, (((1,), (1,)), ((), ())), preferred_element_type=F32)


def _bmm(a, b):
    return lax.dot_general(a, b, (((2,), (1,)), ((0,), (0,))), preferred_element_type=F32)


def _bmm_split(a, b):
    ah = a.astype(BF16)
    al = (a - ah.astype(F32)).astype(BF16)
    bh = b.astype(BF16)
    bl = (b - bh.astype(F32)).astype(BF16)
    return _bmm(jnp.concatenate([ah, al, ah], axis=2), jnp.concatenate([bh, bh, bl], axis=1))


def _bmm_nt(a, b):
    return lax.dot_general(a, b, (((2,), (2,)), ((0,), (0,))), preferred_element_type=F32)


def _bmm_tn(a, b):
    return lax.dot_general(a, b, (((1,), (1,)), ((0,), (0,))), preferred_element_type=F32)


def _split3(x):
    x1 = x.astype(BF16)
    r1 = x - x1.astype(F32)
    x2 = r1.astype(BF16)
    x3 = (r1 - x2.astype(F32)).astype(BF16)
    return x1, x2, x3


def _dot_exact_lhs(mat01, x):
    x1, x2, x3 = _split3(x)
    return _dot(mat01, x1) + _dot(mat01, x2) + _dot(mat01, x3)


def _sigmoid(x):
    return 1.0 / (1.0 + jnp.exp(-x))


def _softplus(x):
    return jnp.maximum(x, 0.0) + jnp.log1p(jnp.exp(-jnp.abs(x)))


def _rms_rows(x, gain):
    return x * lax.rsqrt(jnp.mean(x * x, axis=-1, keepdims=True) + EPS) * gain


def _sub_head_mean_sq(a, gsum_ref):
    return _dot((a * a).astype(BF16), gsum_ref[...])


def _norm_rope_heads(a, ms, gain, cos, sin):
    y = a * lax.rsqrt(ms + EPS) * gain
    lane = lax.broadcasted_iota(jnp.int32, (a.shape[0], LANES), 1)
    first_half = (lane % A_DH) < (A_DH // 2)
    heads = []
    for j in range(A_HEADS):
        t = y[:, j * A_HD:(j + 1) * A_HD]
        partner = jnp.where(first_half,
                            pltpu.roll(t, LANES - A_DH // 2, 1),
                            pltpu.roll(t, A_DH // 2, 1))
        heads.append(t * cos + partner * sin)
    return heads


def _conv_silu(taps, w, cols):
    y = taps[0] * w[0:1, cols]
    for i in range(1, CONV_W):
        y = y + taps[i] * w[i:i + 1, cols]
    return y * _sigmoid(y)


def _store_gdn_tile(t, c, gq_ref, gk_ref, gv_ref):
    group, h = divmod(c, B_HEADS)
    if group < 2:
        t = t * lax.rsqrt(jnp.sum(t * t, axis=-1, keepdims=True) + EPS)
    if group == 0:
        t = t * (B_DK ** -0.5)
    (gq_ref, gk_ref, gv_ref)[group][:, h * B_DK:(h + 1) * B_DK] = t


def _beta_g(z, alog_row, dtb_row):
    lane = lax.broadcasted_iota(jnp.int32, z.shape, 1)
    beta = _sigmoid(z)
    g = -jnp.exp(alog_row) * _softplus(z + dtb_row)
    return jnp.where(lane < B_HEADS, beta, jnp.where(lane < 2 * B_HEADS, g, 0.0))


W_Q0, W_K0, W_V0, W_U0, W_BG0 = 0, A_WIDTH, 2 * A_WIDTH, 3 * A_WIDTH, 3 * A_WIDTH + B_CONV


def _prompt_proj_kernel(x_ref, wn_ref, w_ref, gsum_ref, qn_ref, kn_ref, cos_ref, sin_ref,
                        convw_ref, alog_ref, dtb_ref,
                        q_ref, k_ref, kb_ref, v_ref, vt_ref, gq_ref, gk_ref, gv_ref, bg_ref, conv_ref,
                        ext_ref, *, tiles_per_seq):
    i = pl.program_id(0)
    tm = x_ref.shape[0]
    hb = _rms_rows(x_ref[...], wn_ref[...]).astype(BF16)
    cos = cos_ref[...]
    sin = sin_ref[...]
    w = convw_ref[...]

    @pl.when(i % tiles_per_seq == 0)
    def _():
        ext_ref[0:SUBLANES, :] = jnp.zeros((SUBLANES, B_CONV), F32)

    def conv_tiles(chunk, u_chunk):
        for half in range(2):
            c = 2 * chunk + half
            cols = slice(c * LANES, (c + 1) * LANES)
            taps = [ext_ref[SUBLANES - 3 + t:SUBLANES - 3 + t + tm, cols] for t in range(CONV_W - 1)]
            act = _conv_silu(taps + [u_chunk[:, half * LANES:(half + 1) * LANES]], w, cols)
            _store_gdn_tile(act, c, gq_ref, gk_ref, gv_ref)

    wide = 2 * LANES

    def u_matmul(chunk):
        u_chunk = _dot(hb, w_ref[:, W_U0 + chunk * wide:W_U0 + (chunk + 1) * wide])
        ext_ref[SUBLANES:SUBLANES + tm, chunk * wide:(chunk + 1) * wide] = u_chunk
        return u_chunk

    u0 = u_matmul(0)
    a_q = _dot(hb, w_ref[:, W_Q0:W_K0])
    u1 = u_matmul(1)
    conv_tiles(0, u0)
    ms_q = _sub_head_mean_sq(a_q, gsum_ref)
    u2 = u_matmul(2)
    conv_tiles(1, u1)
    a_k = _dot(hb, w_ref[:, W_K0:W_V0])
    u3 = u_matmul(3)
    conv_tiles(2, u2)
    ms_k = _sub_head_mean_sq(a_k, gsum_ref)
    u4 = u_matmul(4)
    conv_tiles(3, u3)
    v = _dot(hb, w_ref[:, W_V0:W_U0])
    u5 = u_matmul(5)
    conv_tiles(4, u4)
    z = _dot(hb, w_ref[:, W_BG0:])
    conv_tiles(5, u5)
    conv_ref[0] = ext_ref[tm + SUBLANES - (CONV_W - 1):tm + SUBLANES, :]
    ext_ref[0:SUBLANES, :] = ext_ref[tm:tm + SUBLANES, :]
    q_heads = _norm_rope_heads(a_q, ms_q, qn_ref[...], cos, sin)
    for h in range(A_HEADS):
        q_ref[:, h * A_HD:(h + 1) * A_HD] = (q_heads[h] * Q_SCALE).astype(BF16)
    k_heads = _norm_rope_heads(a_k, ms_k, kn_ref[...], cos, sin)
    for h in range(A_HEADS):
        k_ref[pl.ds(h, tm, stride=A_HEADS), :] = k_heads[h]
        kb_ref[:, h * A_HD:(h + 1) * A_HD] = k_heads[h].astype(BF16)
    for h in range(A_HEADS):
        sl = slice(h * A_HD, (h + 1) * A_HD)
        v_ref[pl.ds(h, tm, stride=A_HEADS), :] = v[:, sl]
        vt_ref[0, 0, h] = v[:, sl].T.astype(BF16)
    bg_ref[...] = _beta_g(z, alog_ref[...], dtb_ref[...])


def _sample_proj_kernel(x_ref, wn_ref, w_ref, gsum_ref, qn_ref, kn_ref, cos_ref, sin_ref,
                        convw_ref, alog_ref, dtb_ref, c0_ref, c1_ref, c2_ref,
                        q_ref, k_ref, v_ref, gq_ref, gk_ref, gv_ref, bg_ref, u_ref):
    hb = _rms_rows(x_ref[...], wn_ref[...]).astype(BF16)
    cos = cos_ref[...]
    sin = sin_ref[...]
    a_q = _dot(hb, w_ref[:, W_Q0:W_K0])
    a_k = _dot(hb, w_ref[:, W_K0:W_V0])
    q_heads = _norm_rope_heads(a_q, _sub_head_mean_sq(a_q, gsum_ref), qn_ref[...], cos, sin)
    k_heads = _norm_rope_heads(a_k, _sub_head_mean_sq(a_k, gsum_ref), kn_ref[...], cos, sin)
    for h in range(A_HEADS):
        sl = slice(h * A_HD, (h + 1) * A_HD)
        q_ref[:, sl] = q_heads[h] * Q_SCALE
        k_ref[:, sl] = k_heads[h]
    v_ref[...] = _dot(hb, w_ref[:, W_V0:W_U0])
    u = _dot(hb, w_ref[:, W_U0:W_BG0])
    u_ref[...] = u
    w = convw_ref[...]

    def conv_tile(c):
        cols = slice(c * LANES, (c + 1) * LANES)
        return _conv_silu([c0_ref[:, cols], c1_ref[:, cols], c2_ref[:, cols], u[:, cols]], w, cols)

    for c in range(B_CONV // LANES):
        _store_gdn_tile(conv_tile(c), c, gq_ref, gk_ref, gv_ref)
    bg_ref[...] = _beta_g(_dot(hb, w_ref[:, W_BG0:]), alog_ref[...], dtb_ref[...])


def _full(shape):
    return pl.BlockSpec(shape, lambda *_: (0,) * len(shape))


def _prompt_projection(x2, n_seq, seq, wn, w_proj, gsum, qn, kn, cos, sin, convw, alog, dtb):
    rows = x2.shape[0]
    tm = min(PROJ_ROWS, seq)
    assert seq % tm == 0 and tm % LANES == 0
    tps = seq // tm
    d = x2.shape[1]
    row_spec = lambda width: pl.BlockSpec((tm, width), lambda i: (i, 0))
    head_row_spec = pl.BlockSpec((A_HEADS * tm, A_HD), lambda i: (i, 0))
    pos_spec = pl.BlockSpec((tm, LANES), lambda i: (i % tps, 0))
    out_shape = (
        jax.ShapeDtypeStruct((rows, A_WIDTH), BF16),
        jax.ShapeDtypeStruct((rows * A_HEADS, A_HD), F32),
        jax.ShapeDtypeStruct((rows, A_WIDTH), BF16),
        jax.ShapeDtypeStruct((rows * A_HEADS, A_HD), F32),
        jax.ShapeDtypeStruct((n_seq, tps, A_HEADS, A_HD, tm), BF16),
        jax.ShapeDtypeStruct((rows, B_QK), F32),
        jax.ShapeDtypeStruct((rows, B_QK), F32),
        jax.ShapeDtypeStruct((rows, B_WIDTH), F32),
        jax.ShapeDtypeStruct((rows, LANES), F32),
        jax.ShapeDtypeStruct((n_seq, CONV_W - 1, B_CONV), F32),
    )
    return pl.pallas_call(
        functools.partial(_prompt_proj_kernel, tiles_per_seq=tps),
        grid=(rows // tm,),
        in_specs=[row_spec(d), _full(wn.shape), _full(w_proj.shape), _full(gsum.shape),
                  _full(qn.shape), _full(kn.shape), pos_spec, pos_spec,
                  _full(convw.shape), _full(alog.shape), _full(dtb.shape)],
        out_specs=(row_spec(A_WIDTH), head_row_spec, row_spec(A_WIDTH), head_row_spec,
                   pl.BlockSpec((1, 1, A_HEADS, A_HD, tm), lambda i: (i // tps, i % tps, 0, 0, 0)),
                   row_spec(B_QK), row_spec(B_QK), row_spec(B_WIDTH), row_spec(LANES),
                   pl.BlockSpec((1, CONV_W - 1, B_CONV), lambda i: (i // tps, 0, 0))),
        out_shape=out_shape,
        scratch_shapes=[pltpu.VMEM((tm + SUBLANES, B_CONV), F32)],
        compiler_params=pltpu.CompilerParams(dimension_semantics=("arbitrary",),
                                             vmem_limit_bytes=VMEM_LIMIT_BYTES),
        name="prompt_projection",
    )(x2, wn, w_proj, gsum, qn, kn, cos, sin, convw, alog, dtb)


def _sample_projection(x2, wn, w_proj, gsum, qn, kn, cos, sin, convw, alog, dtb, c0, c1, c2):
    rows = x2.shape[0]
    out_shape = (
        jax.ShapeDtypeStruct((rows, A_WIDTH), F32),
        jax.ShapeDtypeStruct((rows, A_WIDTH), F32),
        jax.ShapeDtypeStruct((rows, A_WIDTH), F32),
        jax.ShapeDtypeStruct((rows, B_QK), F32),
        jax.ShapeDtypeStruct((rows, B_QK), F32),
        jax.ShapeDtypeStruct((rows, B_WIDTH), F32),
        jax.ShapeDtypeStruct((rows, LANES), F32),
        jax.ShapeDtypeStruct((rows, B_CONV), F32),
    )
    args = (x2, wn, w_proj, gsum, qn, kn, cos, sin, convw, alog, dtb, c0, c1, c2)
    return pl.pallas_call(
        _sample_proj_kernel,
        grid=(1,),
        in_specs=[_full(a.shape) for a in args],
        out_specs=tuple(_full(s.shape) for s in out_shape),
        out_shape=out_shape,
        compiler_params=pltpu.CompilerParams(dimension_semantics=("arbitrary",),
                                             vmem_limit_bytes=VMEM_LIMIT_BYTES),
        name="sample_projection",
    )(*args)


def _lambda_value(lamp, lam_init):
    s1 = jnp.sum(lamp[0:1, :] * lamp[1:2, :], axis=1, keepdims=True)
    s2 = jnp.sum(lamp[2:3, :] * lamp[3:4, :], axis=1, keepdims=True)
    return jnp.exp(s1) - jnp.exp(s2) + lam_init


def _dual_attn_kernel(pt_ref, tn_ref, th_ref, tqb_ref, tkv_ref,
                      lamp_ref, subln_ref, q_ref, k_ref, vt_ref, q8_ref, k8_ref, v8_ref, *rest,
                      pages_per_step, paged_steps, chunks_per_seq, lam_init):
    del tn_ref, th_ref
    (ck_ref, cv_ref, o_ref, os_ref, m_ref, acc_ref, sm_ref, sl_ref, sacc_ref,
     kbuf_ref, vbuf_ref, ksem_ref, vsem_ref) = rest
    step = pl.program_id(0)
    qb = tqb_ref[step]
    kv = tkv_ref[step]
    tq = q_ref.shape[1]
    tk = k_ref.shape[1]
    kv_last = lax.div((qb + 1) * tq - 1, tk)
    lam = _lambda_value(lamp_ref[...], lam_init)
    subln = subln_ref[...]
    active = step < paged_steps
    n_buf = kbuf_ref.shape[0]
    slot = lax.rem(step, n_buf)

    def page_copies(page_step):
        buf = lax.rem(page_step, n_buf)
        copies = []
        for j in range(pages_per_step):
            page = pt_ref[page_step * pages_per_step + j]
            copies.append(pltpu.make_async_copy(ck_ref.at[page], kbuf_ref.at[buf, j], ksem_ref.at[buf]))
            copies.append(pltpu.make_async_copy(cv_ref.at[page], vbuf_ref.at[buf, j], vsem_ref.at[buf]))
        return copies

    @pl.when(step == 0)
    def _():
        for ahead in range(min(n_buf - 1, paged_steps)):
            for c in page_copies(ahead):
                c.start()

    @pl.when(step + (n_buf - 1) < paged_steps)
    def _():
        for c in page_copies(step + (n_buf - 1)):
            c.start()

    @pl.when(active)
    def _():
        for c in page_copies(step):
            c.wait()

    @pl.when(kv == 0)
    def _():
        m_ref[...] = jnp.full(m_ref.shape, -jnp.inf, F32)
        acc_ref[...] = jnp.zeros_like(acc_ref)

    chunk = lax.rem(step, chunks_per_seq)
    rows = 2 * A_HEADS
    page_rows = PAGE_SIZE * A_HEADS
    qrow = lax.broadcasted_iota(jnp.int32, (rows, A_HD), 0)
    qlane = lax.broadcasted_iota(jnp.int32, (rows, A_HD), 1)
    qmat = jnp.where(qlane // A_DH == qrow // A_HEADS, q8_ref[0], 0.0)

    @pl.when(jnp.logical_and(active, chunk == 0))
    def _():
        sm_ref[...] = jnp.sum(qmat * k8_ref[0], axis=1, keepdims=True)
        sl_ref[...] = jnp.ones_like(sl_ref)
        sacc_ref[...] = v8_ref[0]

    def dual_step(masked):
        q = q_ref[0]
        lane = lax.broadcasted_iota(jnp.int32, q.shape, 1)
        zero = jnp.zeros_like(q)
        kblk = k_ref[0]
        s_half = [_dot_nt(kblk, jnp.where(lane < A_DH, q, zero)),
                  _dot_nt(kblk, jnp.where(lane >= A_DH, q, zero))]
        qmb = qmat.astype(BF16)
        sc = jnp.concatenate([_dot_nt(qmb, kbuf_ref[slot, j].astype(BF16))
                              for j in range(pages_per_step)], axis=1)
        if masked:
            causal = (lax.broadcasted_iota(jnp.int32, (tk, tq), 0)
                      <= lax.broadcasted_iota(jnp.int32, (tk, tq), 1) + (qb * tq - kv * tk))
        tv = vt_ref.shape[-1]
        ones_rows = jnp.ones((acc_ref.shape[0] - A_HD, tv), BF16)
        for half in range(2):
            cols = slice(half * tq, (half + 1) * tq)
            s = jnp.where(causal, s_half[half], -jnp.inf) if masked else s_half[half]
            m_prev = m_ref[:, cols]
            m_new = jnp.maximum(m_prev, jnp.max(s, axis=0, keepdims=True))
            alpha = jnp.exp2(m_prev - m_new)
            m_ref[:, cols] = m_new
            acc = alpha * acc_ref[:, cols]
            for j in range(vt_ref.shape[1]):
                pb = jnp.exp2(s[j * tv:(j + 1) * tv] - m_new).astype(BF16)
                vt_ext = jnp.concatenate([vt_ref[0, j, 0], ones_rows], axis=0)
                acc = acc + _dot(vt_ext, pb)
            acc_ref[:, cols] = acc
        row = lax.broadcasted_iota(jnp.int32, sc.shape, 0)
        col = lax.broadcasted_iota(jnp.int32, sc.shape, 1)
        sc = jnp.where(col % A_HEADS == row % A_HEADS, sc, -jnp.inf)
        sm_prev = sm_ref[...]
        sl_prev = sl_ref[...]
        sacc_prev = sacc_ref[...]
        sm_new = jnp.maximum(sm_prev, jnp.max(sc, axis=1, keepdims=True))
        salpha = jnp.exp2(sm_prev - sm_new)
        sp = jnp.exp2(sc - sm_new)
        spb = sp.astype(BF16)
        pv = _dot(spb[:, 0:page_rows], vbuf_ref[slot, 0].astype(BF16))
        for j in range(1, pages_per_step):
            pv = pv + _dot(spb[:, j * page_rows:(j + 1) * page_rows], vbuf_ref[slot, j].astype(BF16))
        sl_ref[...] = jnp.where(active, salpha * sl_prev + jnp.sum(sp, axis=1, keepdims=True), sl_prev)
        sacc_ref[...] = jnp.where(active, salpha * sacc_prev + pv, sacc_prev)
        sm_ref[...] = jnp.where(active, sm_new, sm_prev)

    @pl.when(kv < kv_last)
    def _():
        dual_step(False)

    @pl.when(kv == kv_last)
    def _():
        dual_step(True)
        o = acc_ref[0:A_HD, :] / acc_ref[A_HD:A_HD + 1, :]
        out = (o[:, :tq] - lam * o[:, tq:]).T
        o_ref[0] = _rms_rows(out, subln) * (1.0 - lam_init)

    @pl.when(jnp.logical_and(active, chunk == chunks_per_seq - 1))
    def _():
        o = sacc_ref[...] / sl_ref[...]
        out = o[:A_HEADS] - lam * o[A_HEADS:]
        os_ref[0] = _rms_rows(out, subln) * (1.0 - lam_init)


def _dual_attention(q, kb, vt, q_s, k_s, v_s, cache_k, cache_v, page_table, lamp, subln,
                    n_seq, seq, lam_init):
    tv = vt.shape[-1]
    tq = min(ATTN_BLOCK, seq)
    tk = min(ATTN_KEY_BLOCK, seq)
    assert seq % tk == 0 and tk % tq == 0 and tq % tv == 0
    nq = seq // tq
    n_dec, n_pages = page_table.shape
    n_pool = cache_k.shape[0]
    pairs = [(n, h, i, j) for n in range(n_seq) for h in range(A_HEADS)
             for i in range(nq) for j in range(((i + 1) * tq - 1) // tk + 1)]
    steps = len(pairs)
    fits = [p for p in range(1, n_pages + 1) if n_pages % p == 0 and n_dec * (n_pages // p) <= steps]
    assert fits, "prompt attention grid too short to carry the cache page stream"
    pps = fits[0]
    cps = n_pages // pps
    paged_steps = n_dec * cps
    tabs = [jnp.asarray([p[c] for p in pairs], jnp.int32) for c in range(4)]

    q3 = q.reshape(n_seq, seq, A_WIDTH)
    k3 = kb.reshape(n_seq, seq, A_WIDTH)
    ck = cache_k.reshape(n_pool, PAGE_SIZE * A_HEADS, A_HD)
    cv = cache_v.reshape(n_pool, PAGE_SIZE * A_HEADS, A_HD)
    stack2 = lambda a: jnp.tile(a.reshape(n_dec, A_HEADS, A_HD), (1, 2, 1))
    rows = 2 * A_HEADS

    pstep = lambda s: jnp.minimum(s, paged_steps - 1)
    const = lambda shape: pl.BlockSpec(shape, lambda s, *_: (0,) * len(shape))
    q_spec = pl.BlockSpec((1, tq, A_HD), lambda s, pt, tn, th, tqb, tkv: (tn[s], tqb[s], th[s]))
    k_spec = pl.BlockSpec((1, tk, A_HD), lambda s, pt, tn, th, tqb, tkv: (tn[s], tkv[s], th[s]))
    vt_spec = pl.BlockSpec((1, tk // tv, 1, A_HD, tv),
                           lambda s, pt, tn, th, tqb, tkv: (tn[s], tkv[s], th[s], 0, 0))
    row_spec = pl.BlockSpec((1, rows, A_HD), lambda s, *_: (pstep(s) // cps, 0, 0))
    hbm_spec = pl.BlockSpec(memory_space=pl.ANY)
    page_buf = pltpu.VMEM((PAGE_RING, pps, PAGE_SIZE * A_HEADS, A_HD), F32)

    grid_spec = pltpu.PrefetchScalarGridSpec(
        num_scalar_prefetch=5,
        grid=(steps,),
        in_specs=[const(lamp.shape), const(subln.shape), q_spec, k_spec, vt_spec,
                  row_spec, row_spec, row_spec, hbm_spec, hbm_spec],
        out_specs=(q_spec, pl.BlockSpec((1, A_HEADS, A_HD), lambda s, *_: (pstep(s) // cps, 0, 0))),
        scratch_shapes=[pltpu.VMEM((1, 2 * tq), F32),
                        pltpu.VMEM((A_HD + DENOM_ROWS, 2 * tq), F32),
                        pltpu.VMEM((rows, 1), F32), pltpu.VMEM((rows, 1), F32),
                        pltpu.VMEM((rows, A_HD), F32),
                        page_buf, page_buf,
                        pltpu.SemaphoreType.DMA((PAGE_RING,)), pltpu.SemaphoreType.DMA((PAGE_RING,))],
    )
    o_p, o_s = pl.pallas_call(
        functools.partial(_dual_attn_kernel, pages_per_step=pps, paged_steps=paged_steps,
                          chunks_per_seq=cps, lam_init=lam_init),
        grid_spec=grid_spec,
        out_shape=(jax.ShapeDtypeStruct((n_seq, seq, A_WIDTH), F32),
                   jax.ShapeDtypeStruct((n_dec, A_HEADS, A_HD), F32)),
        compiler_params=pltpu.CompilerParams(dimension_semantics=("arbitrary",),
                                             vmem_limit_bytes=VMEM_LIMIT_BYTES),
        name="dual_attention",
    )(page_table.reshape(-1), *tabs, lamp, subln, q3, k3, vt, stack2(q_s), stack2(k_s), stack2(v_s), ck, cv)
    return o_p.reshape(n_seq * seq, A_WIDTH), o_s.reshape(n_dec, A_WIDTH)


def _gdn_chunk_kernel(gq_ref, gk_ref, gv_ref, bg_ref, onorm_ref, o_ref, s_out_ref, s_ref):
    tb = pl.program_id(0)
    n_seq, tt, _ = gq_ref.shape
    nc = tt // CHUNK

    @pl.when(tb == 0)
    def _():
        s_ref[...] = jnp.zeros_like(s_ref)

    ii = lax.broadcasted_iota(jnp.int32, (CHUNK, CHUNK), 0)
    jj = lax.broadcasted_iota(jnp.int32, (CHUNK, CHUNK), 1)
    incl = ii >= jj
    strict = ii > jj
    ltri = incl.astype(BF16)
    onorm = onorm_ref[...]

    heads = [(n, h) for n in range(n_seq) for h in range(B_HEADS)]
    n_heads = len(heads)
    group = GDN_CHUNK_GROUP if nc % GDN_CHUNK_GROUP == 0 else 1
    chains = [(cc, n, h) for cc in range(group) for n, h in heads]

    def chunk_group_step(gi, carry):
        rows = [pl.ds(pl.multiple_of((gi * group + cc) * CHUNK, CHUNK), CHUNK) for cc in range(group)]
        head_cols = lambda ref: jnp.stack([ref[n, rows[cc], h * B_DK:(h + 1) * B_DK] for cc, n, h in chains])
        q = head_cols(gq_ref)
        k = head_cols(gk_ref)
        v = head_cols(gv_ref)
        bgc = {(cc, n): bg_ref[n, rows[cc], :] for cc in range(group) for n in range(n_seq)}
        gcum = {key: _dot_exact_lhs(ltri, b) for key, b in bgc.items()}
        lane_bcast = lambda x, c: jnp.broadcast_to(x[:, c:c + 1], (CHUNK, B_DK))
        bh = jnp.stack([lane_bcast(bgc[cc, n], h) for cc, n, h in chains])
        gcs = [lane_bcast(gcum[cc, n], B_HEADS + h) for cc, n, h in chains]
        gc = jnp.stack(gcs)
        d = gc[:, :, :CHUNK] - jnp.stack([g.T[:CHUNK, :] for g in gcs])
        decay = jnp.exp(jnp.where(incl[None], d, -jnp.inf))
        glast = gc[:, CHUNK - 1:CHUNK, :]
        eg = jnp.exp(gc)
        kb = k * bh
        kbf = k.astype(BF16)
        m = -jnp.where(strict[None], _bmm_nt(kb.astype(BF16), kbf) * decay, 0.0)
        sol = jnp.concatenate([v * bh, kb * eg], axis=2)
        mm = lambda level, a, b: (_bmm_split(a, b) if level < SOLVE_SPLIT_LEVELS
                                  else _bmm(a.astype(BF16), b.astype(BF16)))
        sol = sol + mm(0, m, sol)
        p = m
        for level in range(5):
            p = mm(level, p, p)
            sol = sol + mm(level + 1, p, sol)
        u = sol[:, :, :B_DV]
        wb = sol[:, :, B_DV:].astype(BF16)
        qk = jnp.where(incl[None], _bmm_nt(q.astype(BF16), kbf) * decay, 0.0).astype(BF16)
        qg = (q * eg).astype(BF16)
        kg = (k * jnp.exp(glast - gc)).astype(BF16)
        gl = jnp.exp(glast)
        for cc in range(group):
            part = slice(cc * n_heads, (cc + 1) * n_heads)
            s = s_ref[...]
            sb = s.astype(BF16)
            vn = (u[part] - _bmm(wb[part], sb)).astype(BF16)
            o = _bmm(qg[part], sb) + _bmm(qk[part], vn)
            s_ref[...] = s * gl[part] + _bmm_tn(kg[part], vn)
            on = _rms_rows(o, onorm)
            for i, (n, h) in enumerate(heads):
                o_ref[n, rows[cc], h * B_DV:(h + 1) * B_DV] = on[i]
        return carry

    lax.fori_loop(0, nc // group, chunk_group_step, 0)

    @pl.when(tb == pl.num_programs(0) - 1)
    def _():
        s_out_ref[...] = s_ref[...]


def _gdn_chunked(gq, gk, gv, bg, onorm, n_seq, seq):
    tt = min(GDN_ROWS, seq)
    assert seq % tt == 0 and tt % CHUNK == 0
    blk = lambda width: pl.BlockSpec((n_seq, tt, width), lambda t: (0, t, 0))
    state_shape = (n_seq * B_HEADS, B_DK, B_DV)
    out_shape = (jax.ShapeDtypeStruct((n_seq, seq, B_WIDTH), F32),
                 jax.ShapeDtypeStruct(state_shape, F32))
    o, s = pl.pallas_call(
        _gdn_chunk_kernel,
        grid=(seq // tt,),
        in_specs=[blk(B_QK), blk(B_QK), blk(B_WIDTH), blk(LANES), _full(onorm.shape)],
        out_specs=(blk(B_WIDTH), _full(state_shape)),
        out_shape=out_shape,
        scratch_shapes=[pltpu.VMEM(state_shape, F32)],
        compiler_params=pltpu.CompilerParams(dimension_semantics=("arbitrary",),
                                             vmem_limit_bytes=VMEM_LIMIT_BYTES),
        name="gdn_chunked",
    )(gq.reshape(n_seq, seq, B_QK), gk.reshape(n_seq, seq, B_QK), gv.reshape(n_seq, seq, B_WIDTH),
      bg.reshape(n_seq, seq, LANES), onorm)
    return o.reshape(n_seq * seq, B_WIDTH), s.reshape(n_seq, B_HEADS, B_DK, B_DV)


def _gdn_step_kernel(gq_ref, gk_ref, gv_ref, bg_ref, onorm_ref, s_ref, o_ref, so_ref):
    ri = lax.broadcasted_iota(jnp.int32, (B_DK, B_DK), 0)
    ci = lax.broadcasted_iota(jnp.int32, (B_DK, B_DK), 1)
    eye = (ri == ci).astype(F32)
    onorm = onorm_ref[...]
    for b in range(gq_ref.shape[0]):
        bg = bg_ref[b]
        for h in range(B_HEADS):
            sl = slice(h * B_DK, (h + 1) * B_DK)
            q = gq_ref[b][:, sl]
            k = gk_ref[b][:, sl]
            v = gv_ref[b][:, sl]
            beta = bg[:, h:h + 1]
            a = jnp.exp(bg[:, B_HEADS + h:B_HEADS + h + 1])
            kcol = jnp.sum(eye * k, axis=1, keepdims=True)
            qcol = jnp.sum(eye * q, axis=1, keepdims=True)
            s = s_ref[b, h] * a
            u = beta * (v - jnp.sum(s * kcol, axis=0, keepdims=True))
            s = s + kcol * u
            so_ref[b, h] = s
            o = jnp.sum(s * qcol, axis=0, keepdims=True)
            o_ref[b, :, sl] = _rms_rows(o, onorm)


def _gdn_step(gq, gk, gv, bg, onorm, state):
    n_dec = state.shape[0]
    bb = max(p for p in range(1, GDN_STEP_SEQS + 1) if n_dec % p == 0)
    row_spec = lambda width: pl.BlockSpec((bb, 1, width), lambda i: (i, 0, 0))
    s_spec = pl.BlockSpec((bb, B_HEADS, B_DK, B_DV), lambda i: (i, 0, 0, 0))
    out_shape = (jax.ShapeDtypeStruct((n_dec, 1, B_WIDTH), F32),
                 jax.ShapeDtypeStruct(state.shape, F32))
    o, s_new = pl.pallas_call(
        _gdn_step_kernel,
        grid=(n_dec // bb,),
        in_specs=[row_spec(B_QK), row_spec(B_QK), row_spec(B_WIDTH), row_spec(LANES),
                  pl.BlockSpec(onorm.shape, lambda i: (0, 0)), s_spec],
        out_specs=(row_spec(B_WIDTH), s_spec),
        out_shape=out_shape,
        compiler_params=pltpu.CompilerParams(dimension_semantics=("arbitrary",),
                                             vmem_limit_bytes=VMEM_LIMIT_BYTES),
        name="gdn_step",
    )(gq.reshape(n_dec, 1, B_QK), gk.reshape(n_dec, 1, B_QK), gv.reshape(n_dec, 1, B_WIDTH),
      bg.reshape(n_dec, 1, LANES), onorm, state.astype(F32))
    return o.reshape(n_dec, B_WIDTH), s_new


def _merge_out_kernel(x_ref, wn_ref, wg_ref, oa_ref, ob_ref, wua_ref, wub_ref, wo_ref, y_ref):
    x = x_ref[...]
    hb = _rms_rows(x, wn_ref[...]).astype(BF16)
    az = _dot(hb, wg_ref[:, 0:A_WIDTH])
    bz = _dot(hb, wg_ref[:, A_WIDTH:A_WIDTH + B_WIDTH])
    oa = oa_ref[...] * (az * _sigmoid(az))
    ob = ob_ref[...] * (bz * _sigmoid(bz))
    d = x.shape[1]
    g0 = A_WIDTH + B_WIDTH
    ga = _dot(hb, wg_ref[:, g0:g0 + d])
    gb = _dot(hb, wg_ref[:, g0 + d:g0 + 2 * d])
    y = (_sigmoid(ga) * _dot(oa.astype(BF16), wua_ref[...])
         + _sigmoid(gb) * _dot(ob.astype(BF16), wub_ref[...]))
    y_ref[...] = x + _dot(y.astype(BF16), wo_ref[...])


def _merge_out(x2, wn, w_gate, o_a, o_b, w_up_a, w_up_b, w_out, tm):
    rows, d = x2.shape
    assert rows % tm == 0
    row_spec = lambda width: pl.BlockSpec((tm, width), lambda i: (i, 0))
    return pl.pallas_call(
        _merge_out_kernel,
        grid=(rows // tm,),
        in_specs=[row_spec(d), _full(wn.shape), _full(w_gate.shape), row_spec(A_WIDTH), row_spec(B_WIDTH),
                  _full(w_up_a.shape), _full(w_up_b.shape), _full(w_out.shape)],
        out_specs=row_spec(d),
        out_shape=jax.ShapeDtypeStruct((rows, d), F32),
        compiler_params=pltpu.CompilerParams(dimension_semantics=("arbitrary",),
                                             vmem_limit_bytes=VMEM_LIMIT_BYTES),
        name="merge_out",
    )(x2, wn, w_gate, o_a, o_b, w_up_a, w_up_b, w_out)


def _rope_tables(pos):
    half = A_DH // 2
    inv = ROPE_THETA ** (-jnp.arange(half, dtype=F32) / half)
    ang = pos.astype(F32)[:, None] * inv[None, :]
    cos = jnp.cos(ang)
    sin = jnp.sin(ang)
    return jnp.tile(cos, (1, 4)), jnp.concatenate([-sin, sin, -sin, sin], axis=1)


def _repack_w_in(w):
    d = w.shape[0]
    o = 0
    parts = {}
    for name, size in (("aq", A_WIDTH), ("ak", A_WIDTH), ("av", A_WIDTH), ("az", A_WIDTH),
                       ("bq", B_QK), ("bk", B_QK), ("bv", B_WIDTH), ("bz", B_WIDTH),
                       ("bb", B_HEADS), ("ba", B_HEADS), ("ga", d), ("gb", d)):
        parts[name] = w[:, o:o + size]
        o += size
    assert o == w.shape[1]
    pad = jnp.zeros((d, LANES - 2 * B_HEADS), w.dtype)
    w_proj = jnp.concatenate([parts[n] for n in ("aq", "ak", "av", "bq", "bk", "bv", "bb", "ba")] + [pad], axis=1)
    w_gate = jnp.concatenate([parts[n] for n in ("az", "bz", "ga", "gb")], axis=1)
    return w_proj.astype(BF16), w_gate.astype(BF16)


def _lane_row(vec, offset):
    return jnp.zeros((1, LANES), F32).at[0, offset:offset + vec.shape[0]].set(vec.astype(F32))


def kernel(x_prompt, x_sample, cache_k, cache_v, state_conv, state_ssm, page_table, w_norm, w_in, a_qn, a_kn, a_lq1, a_lk1, a_lq2, a_lk2, a_subln, conv_w, a_log, dt_bias, b_onorm, w_up_a, w_up_b, w_out):
    depth = w_norm.shape[0]
    n_seq, seq, d = x_prompt.shape
    n_dec, dec_seq, _ = x_sample.shape
    assert dec_seq == 1, "the sample path handles one new token per sequence"
    n_pages = page_table.shape[1]
    past_len = n_pages * PAGE_SIZE

    cos_p, sin_p = _rope_tables(jnp.arange(seq, dtype=jnp.int32))
    cos_s, sin_s = _rope_tables(past_len + jnp.arange(dec_seq, dtype=jnp.int32))
    sub = lax.broadcasted_iota(jnp.int32, (A_WIDTH, A_WIDTH), 0) // A_DH
    gsum = jnp.where(sub == sub.T, 1.0 / A_DH, 0.0).astype(BF16)

    hp = x_prompt.reshape(n_seq * seq, d)
    hs = x_sample.reshape(n_dec, d)
    outs = [[] for _ in range(8)]
    for l in range(depth):
        lam_init = 0.8 - 0.6 * math.exp(-0.3 * l)
        wn = w_norm[l].reshape(1, d)
        w_proj, w_gate = _repack_w_in(w_in[l])
        qn = jnp.tile(a_qn[l], A_WIDTH // A_DH).reshape(1, A_WIDTH)
        kn = jnp.tile(a_kn[l], A_WIDTH // A_DH).reshape(1, A_WIDTH)
        lamp = jnp.stack([a_lq1[l], a_lk1[l], a_lq2[l], a_lk2[l]]).astype(F32)
        subln = a_subln[l].reshape(1, A_HD)
        alog = _lane_row(a_log[l], B_HEADS)
        dtb = _lane_row(dt_bias[l], B_HEADS)
        onorm = b_onorm[l].reshape(1, B_DV)
        wua = w_up_a[l].astype(BF16)
        wub = w_up_b[l].astype(BF16)
        wo = w_out[l].astype(BF16)
        convw = conv_w[l]

        q, k, kb, v, vt, gq, gk, gv, bg, conv_p = _prompt_projection(
            hp, n_seq, seq, wn, w_proj, gsum, qn, kn, cos_p, sin_p, convw, alog, dtb)
        sc = state_conv[l]
        qs, ks, vs, gqs, gks, gvs, bgs, us = _sample_projection(
            hs, wn, w_proj, gsum, qn, kn, cos_s, sin_s, convw, alog, dtb,
            sc[:, 0, :], sc[:, 1, :], sc[:, 2, :])
        o_a, o_as = _dual_attention(q, kb, vt, qs, ks, vs, cache_k[l], cache_v[l], page_table,
                                    lamp, subln, n_seq, seq, lam_init)
        o_b, ssm_p = _gdn_chunked(gq, gk, gv, bg, onorm, n_seq, seq)
        hp = _merge_out(hp, wn, w_gate, o_a, o_b, wua, wub, wo, min(MERGE_ROWS, seq))
        o_bs, ssm_s = _gdn_step(gqs, gks, gvs, bgs, onorm, state_ssm[l])
        hs = _merge_out(hs, wn, w_gate, o_as, o_bs, wua, wub, wo, n_dec)

        outs[0].append(k.reshape(n_seq, seq, A_HEADS, A_HD))
        outs[1].append(v.reshape(n_seq, seq, A_HEADS, A_HD))
        outs[2].append(conv_p)
        outs[3].append(ssm_p.astype(x_prompt.dtype))
        outs[4].append(ks.reshape(n_dec, dec_seq, A_HEADS, A_HD))
        outs[5].append(vs.reshape(n_dec, dec_seq, A_HEADS, A_HD))
        outs[6].append(jnp.concatenate([sc[:, 1:, :], us[:, None, :]], axis=1))
        outs[7].append(ssm_s.astype(x_sample.dtype))

    return (hp.reshape(n_seq, seq, d), hs.reshape(n_dec, dec_seq, d)) + tuple(jnp.stack(o) for o in outs)
```

```python
import functools
import math

import jax
import jax.numpy as jnp
from jax import lax
from jax.experimental import pallas as pl
from jax.experimental.pallas import tpu as pltpu

F32 = jnp.float32
BF16 = jnp.bfloat16

A_HEADS = 4
A_DH = 64
A_HD = 2 * A_DH
A_WIDTH = A_HEADS * A_HD
B_HEADS = 4
B_DK = 128
B_DV = 128
B_QK = B_HEADS * B_DK
B_WIDTH = B_HEADS * B_DV
B_CONV = 2 * B_QK + B_WIDTH
CONV_W = 4
CHUNK = 64
PAGE_SIZE = 128
ROPE_THETA = 10000.0
EPS = 1e-6

LANES = 128
SUBLANES = 8
VMEM_LIMIT_BYTES = 56 * 1024 * 1024

PROJ_ROWS = 512
ATTN_BLOCK = 1024
ATTN_KEY_BLOCK = 1024
MERGE_ROWS = 512
GDN_ROWS = 256
PAGE_RING = 2
GDN_STEP_SEQS = 4
GDN_CHUNK_GROUP = 4
SOLVE_SPLIT_LEVELS = 3
DENOM_ROWS = 16
Q_SCALE = (A_DH ** -0.5) * math.log2(math.e)


def _dot(a, b):
    return jnp.dot(a, b, preferred_element_type=F32)


def _dot_nt(a, b):
    return lax.dot_general(a, b, (((1,), (1,)), ((), ())), preferred_element_type=F32)


def _bmm(a, b):
    return lax.dot_general(a, b, (((2,), (1,)), ((0,), (0,))), preferred_element_type=F32)


def _bmm_split(a, b):
    ah = a.astype(BF16)
    al = (a - ah.astype(F32)).astype(BF16)
    bh = b.astype(BF16)
    bl = (b - bh.astype(F32)).astype(BF16)
    return _bmm(jnp.concatenate([ah, al, ah], axis=2), jnp.concatenate([bh, bh, bl], axis=1))


def _bmm_nt(a, b):
    return lax.dot_general(a, b, (((2,), (2,)), ((0,), (0,))), preferred_element_type=F32)


def _bmm_tn(a, b):
    return lax.dot_general(a, b, (((1,), (1,)), ((0,), (0,))), preferred_element_type=F32)


def _split3(x):
    x1 = x.astype(BF16)
    r1 = x - x1.astype(F32)
    x2 = r1.astype(BF16)
    x3 = (r1 - x2.astype(F32)).astype(BF16)
    return x1, x2, x3


def _dot_exact_lhs(mat01, x):
    x1, x2, x3 = _split3(x)
    return _dot(mat01, x1) + _dot(mat01, x2) + _dot(mat01, x3)


def _sigmoid(x):
    return 1.0 / (1.0 + jnp.exp(-x))


def _softplus(x):
    return jnp.maximum(x, 0.0) + jnp.log1p(jnp.exp(-jnp.abs(x)))


def _rms_rows(x, gain):
    return x * lax.rsqrt(jnp.mean(x * x, axis=-1, keepdims=True) + EPS) * gain


def _sub_head_mean_sq(a, gsum_ref):
    return _dot((a * a).astype(BF16), gsum_ref[...])


def _norm_rope_heads(a, ms, gain, cos, sin):
    y = a * lax.rsqrt(ms + EPS) * gain
    lane = lax.broadcasted_iota(jnp.int32, (a.shape[0], LANES), 1)
    first_half = (lane % A_DH) < (A_DH // 2)
    heads = []
    for j in range(A_HEADS):
        t = y[:, j * A_HD:(j + 1) * A_HD]
        partner = jnp.where(first_half,
                            pltpu.roll(t, LANES - A_DH // 2, 1),
                            pltpu.roll(t, A_DH // 2, 1))
        heads.append(t * cos + partner * sin)
    return heads


def _conv_silu(taps, w, cols):
    y = taps[0] * w[0:1, cols]
    for i in range(1, CONV_W):
        y = y + taps[i] * w[i:i + 1, cols]
    return y * _sigmoid(y)


def _store_gdn_tile(t, c, gq_ref, gk_ref, gv_ref):
    group, h = divmod(c, B_HEADS)
    if group < 2:
        t = t * lax.rsqrt(jnp.sum(t * t, axis=-1, keepdims=True) + EPS)
    if group == 0:
        t = t * (B_DK ** -0.5)
    (gq_ref, gk_ref, gv_ref)[group][:, h * B_DK:(h + 1) * B_DK] = t


def _beta_g(z, alog_row, dtb_row):
    lane = lax.broadcasted_iota(jnp.int32, z.shape, 1)
    beta = _sigmoid(z)
    g = -jnp.exp(alog_row) * _softplus(z + dtb_row)
    return jnp.where(lane < B_HEADS, beta, jnp.where(lane < 2 * B_HEADS, g, 0.0))


W_Q0, W_K0, W_V0, W_U0, W_BG0 = 0, A_WIDTH, 2 * A_WIDTH, 3 * A_WIDTH, 3 * A_WIDTH + B_CONV


def _prompt_proj_kernel(x_ref, wn_ref, w_ref, gsum_ref, qn_ref, kn_ref, cos_ref, sin_ref,
                        convw_ref, alog_ref, dtb_ref,
                        q_ref, k_ref, kb_ref, v_ref, vt_ref, gq_ref, gk_ref, gv_ref, bg_ref, conv_ref,
                        ext_ref, *, tiles_per_seq):
    i = pl.program_id(0)
    tm = x_ref.shape[0]
    hb = _rms_rows(x_ref[...], wn_ref[...]).astype(BF16)
    cos = cos_ref[...]
    sin = sin_ref[...]
    w = convw_ref[...]

    @pl.when(i % tiles_per_seq == 0)
    def _():
        ext_ref[0:SUBLANES, :] = jnp.zeros((SUBLANES, B_CONV), F32)

    def conv_tiles(chunk, u_chunk):
        for half in range(2):
            c = 2 * chunk + half
            cols = slice(c * LANES, (c + 1) * LANES)
            taps = [ext_ref[SUBLANES - 3 + t:SUBLANES - 3 + t + tm, cols] for t in range(CONV_W - 1)]
            act = _conv_silu(taps + [u_chunk[:, half * LANES:(half + 1) * LANES]], w, cols)
            _store_gdn_tile(act, c, gq_ref, gk_ref, gv_ref)

    wide = 2 * LANES

    def u_matmul(chunk):
        u_chunk = _dot(hb, w_ref[:, W_U0 + chunk * wide:W_U0 + (chunk + 1) * wide])
        ext_ref[SUBLANES:SUBLANES + tm, chunk * wide:(chunk + 1) * wide] = u_chunk
        return u_chunk

    def rope_q():
        q_heads = _norm_rope_heads(a_q, ms_q, qn_ref[...], cos, sin)
        for h in range(A_HEADS):
            q_ref[:, h * A_HD:(h + 1) * A_HD] = (q_heads[h] * Q_SCALE).astype(BF16)

    def rope_k():
        k_heads = _norm_rope_heads(a_k, ms_k, kn_ref[...], cos, sin)
        for h in range(A_HEADS):
            k_ref[pl.ds(h, tm, stride=A_HEADS), :] = k_heads[h]
            kb_ref[:, h * A_HD:(h + 1) * A_HD] = k_heads[h].astype(BF16)

    def store_v():
        for h in range(A_HEADS):
            sl = slice(h * A_HD, (h + 1) * A_HD)
            v_ref[pl.ds(h, tm, stride=A_HEADS), :] = v[:, sl]
            vt_ref[0, 0, h] = v[:, sl].T.astype(BF16)

    u0 = u_matmul(0)
    a_q = _dot(hb, w_ref[:, W_Q0:W_K0])
    u1 = u_matmul(1)
    conv_tiles(0, u0)
    ms_q = _sub_head_mean_sq(a_q, gsum_ref)
    u2 = u_matmul(2)
    conv_tiles(1, u1)
    a_k = _dot(hb, w_ref[:, W_K0:W_V0])
    u3 = u_matmul(3)
    conv_tiles(2, u2)
    ms_k = _sub_head_mean_sq(a_k, gsum_ref)
    u4 = u_matmul(4)
    conv_tiles(3, u3)
    v = _dot(hb, w_ref[:, W_V0:W_U0])
    u5 = u_matmul(5)
    conv_tiles(4, u4)
    z = _dot(hb, w_ref[:, W_BG0:])
    conv_tiles(5, u5)
    conv_ref[0] = ext_ref[tm + SUBLANES - (CONV_W - 1):tm + SUBLANES, :]
    ext_ref[0:SUBLANES, :] = ext_ref[tm:tm + SUBLANES, :]
    rope_q()
    rope_k()
    store_v()
    bg_ref[...] = _beta_g(z, alog_ref[...], dtb_ref[...])


def _sample_proj_kernel(x_ref, wn_ref, w_ref, gsum_ref, qn_ref, kn_ref, cos_ref, sin_ref,
                        convw_ref, alog_ref, dtb_ref, c0_ref, c1_ref, c2_ref,
                        q_ref, k_ref, v_ref, gq_ref, gk_ref, gv_ref, bg_ref, u_ref):
    hb = _rms_rows(x_ref[...], wn_ref[...]).astype(BF16)
    cos = cos_ref[...]
    sin = sin_ref[...]
    a_q = _dot(hb, w_ref[:, W_Q0:W_K0])
    a_k = _dot(hb, w_ref[:, W_K0:W_V0])
    q_heads = _norm_rope_heads(a_q, _sub_head_mean_sq(a_q, gsum_ref), qn_ref[...], cos, sin)
    k_heads = _norm_rope_heads(a_k, _sub_head_mean_sq(a_k, gsum_ref), kn_ref[...], cos, sin)
    for h in range(A_HEADS):
        sl = slice(h * A_HD, (h + 1) * A_HD)
        q_ref[:, sl] = q_heads[h] * Q_SCALE
        k_ref[:, sl] = k_heads[h]
    v_ref[...] = _dot(hb, w_ref[:, W_V0:W_U0])
    u = _dot(hb, w_ref[:, W_U0:W_BG0])
    u_ref[...] = u
    w = convw_ref[...]

    def conv_tile(c):
        cols = slice(c * LANES, (c + 1) * LANES)
        return _conv_silu([c0_ref[:, cols], c1_ref[:, cols], c2_ref[:, cols], u[:, cols]], w, cols)

    for c in range(B_CONV // LANES):
        _store_gdn_tile(conv_tile(c), c, gq_ref, gk_ref, gv_ref)
    bg_ref[...] = _beta_g(_dot(hb, w_ref[:, W_BG0:]), alog_ref[...], dtb_ref[...])


def _full(shape):
    return pl.BlockSpec(shape, lambda *_: (0,) * len(shape))


def _prompt_projection(x2, n_seq, seq, wn, w_proj, gsum, qn, kn, cos, sin, convw, alog, dtb):
    rows = x2.shape[0]
    tm = min(PROJ_ROWS, seq)
    assert seq % tm == 0 and tm % LANES == 0
    tps = seq // tm
    d = x2.shape[1]
    row_spec = lambda width: pl.BlockSpec((tm, width), lambda i: (i, 0))
    head_row_spec = pl.BlockSpec((A_HEADS * tm, A_HD), lambda i: (i, 0))
    pos_spec = pl.BlockSpec((tm, LANES), lambda i: (i % tps, 0))
    out_shape = (
        jax.ShapeDtypeStruct((rows, A_WIDTH), BF16),
        jax.ShapeDtypeStruct((rows * A_HEADS, A_HD), F32),
        jax.ShapeDtypeStruct((rows, A_WIDTH), BF16),
        jax.ShapeDtypeStruct((rows * A_HEADS, A_HD), F32),
        jax.ShapeDtypeStruct((n_seq, tps, A_HEADS, A_HD, tm), BF16),
        jax.ShapeDtypeStruct((rows, B_QK), F32),
        jax.ShapeDtypeStruct((rows, B_QK), F32),
        jax.ShapeDtypeStruct((rows, B_WIDTH), F32),
        jax.ShapeDtypeStruct((rows, LANES), F32),
        jax.ShapeDtypeStruct((n_seq, CONV_W - 1, B_CONV), F32),
    )
    return pl.pallas_call(
        functools.partial(_prompt_proj_kernel, tiles_per_seq=tps),
        grid=(rows // tm,),
        in_specs=[row_spec(d), _full(wn.shape), _full(w_proj.shape), _full(gsum.shape),
                  _full(qn.shape), _full(kn.shape), pos_spec, pos_spec,
                  _full(convw.shape), _full(alog.shape), _full(dtb.shape)],
        out_specs=(row_spec(A_WIDTH), head_row_spec, row_spec(A_WIDTH), head_row_spec,
                   pl.BlockSpec((1, 1, A_HEADS, A_HD, tm), lambda i: (i // tps, i % tps, 0, 0, 0)),
                   row_spec(B_QK), row_spec(B_QK), row_spec(B_WIDTH), row_spec(LANES),
                   pl.BlockSpec((1, CONV_W - 1, B_CONV), lambda i: (i // tps, 0, 0))),
        out_shape=out_shape,
        scratch_shapes=[pltpu.VMEM((tm + SUBLANES, B_CONV), F32)],
        compiler_params=pltpu.CompilerParams(dimension_semantics=("arbitrary",),
                                             vmem_limit_bytes=VMEM_LIMIT_BYTES),
        name="prompt_projection",
    )(x2, wn, w_proj, gsum, qn, kn, cos, sin, convw, alog, dtb)


def _sample_projection(x2, wn, w_proj, gsum, qn, kn, cos, sin, convw, alog, dtb, c0, c1, c2):
    rows = x2.shape[0]
    out_shape = (
        jax.ShapeDtypeStruct((rows, A_WIDTH), F32),
        jax.ShapeDtypeStruct((rows, A_WIDTH), F32),
        jax.ShapeDtypeStruct((rows, A_WIDTH), F32),
        jax.ShapeDtypeStruct((rows, B_QK), F32),
        jax.ShapeDtypeStruct((rows, B_QK), F32),
        jax.ShapeDtypeStruct((rows, B_WIDTH), F32),
        jax.ShapeDtypeStruct((rows, LANES), F32),
        jax.ShapeDtypeStruct((rows, B_CONV), F32),
    )
    args = (x2, wn, w_proj, gsum, qn, kn, cos, sin, convw, alog, dtb, c0, c1, c2)
    return pl.pallas_call(
        _sample_proj_kernel,
        grid=(1,),
        in_specs=[_full(a.shape) for a in args],
        out_specs=tuple(_full(s.shape) for s in out_shape),
        out_shape=out_shape,
        compiler_params=pltpu.CompilerParams(dimension_semantics=("arbitrary",),
                                             vmem_limit_bytes=VMEM_LIMIT_BYTES),
        name="sample_projection",
    )(*args)


def _lambda_value(lamp, lam_init):
    s1 = jnp.sum(lamp[0:1, :] * lamp[1:2, :], axis=1, keepdims=True)
    s2 = jnp.sum(lamp[2:3, :] * lamp[3:4, :], axis=1, keepdims=True)
    return jnp.exp(s1) - jnp.exp(s2) + lam_init


def _dual_attn_kernel(pt_ref, tn_ref, th_ref, tqb_ref, tkv_ref,
                      lamp_ref, subln_ref, q_ref, k_ref, vt_ref, q8_ref, k8_ref, v8_ref, *rest,
                      pages_per_step, paged_steps, chunks_per_seq, lam_init):
    del tn_ref, th_ref
    (ck_ref, cv_ref, o_ref, os_ref, m_ref, acc_ref, sm_ref, sl_ref, sacc_ref,
     kbuf_ref, vbuf_ref, ksem_ref, vsem_ref) = rest
    step = pl.program_id(0)
    qb = tqb_ref[step]
    kv = tkv_ref[step]
    tq = q_ref.shape[1]
    tk = k_ref.shape[1]
    kv_last = lax.div((qb + 1) * tq - 1, tk)
    lam = _lambda_value(lamp_ref[...], lam_init)
    subln = subln_ref[...]
    active = step < paged_steps
    n_buf = kbuf_ref.shape[0]
    slot = lax.rem(step, n_buf)

    def page_copies(page_step):
        buf = lax.rem(page_step, n_buf)
        copies = []
        for j in range(pages_per_step):
            page = pt_ref[page_step * pages_per_step + j]
            copies.append(pltpu.make_async_copy(ck_ref.at[page], kbuf_ref.at[buf, j], ksem_ref.at[buf]))
            copies.append(pltpu.make_async_copy(cv_ref.at[page], vbuf_ref.at[buf, j], vsem_ref.at[buf]))
        return copies

    @pl.when(step == 0)
    def _():
        for ahead in range(min(n_buf - 1, paged_steps)):
            for c in page_copies(ahead):
                c.start()

    @pl.when(step + (n_buf - 1) < paged_steps)
    def _():
        for c in page_copies(step + (n_buf - 1)):
            c.start()

    @pl.when(active)
    def _():
        for c in page_copies(step):
            c.wait()

    @pl.when(kv == 0)
    def _():
        m_ref[...] = jnp.full(m_ref.shape, -jnp.inf, F32)
        acc_ref[...] = jnp.zeros_like(acc_ref)

    chunk = lax.rem(step, chunks_per_seq)
    rows = 2 * A_HEADS
    page_rows = PAGE_SIZE * A_HEADS
    qrow = lax.broadcasted_iota(jnp.int32, (rows, A_HD), 0)
    qlane = lax.broadcasted_iota(jnp.int32, (rows, A_HD), 1)
    qmat = jnp.where(qlane // A_DH == qrow // A_HEADS, q8_ref[0], 0.0)

    @pl.when(jnp.logical_and(active, chunk == 0))
    def _():
        sm_ref[...] = jnp.sum(qmat * k8_ref[0], axis=1, keepdims=True)
        sl_ref[...] = jnp.ones_like(sl_ref)
        sacc_ref[...] = v8_ref[0]

    def dual_step(masked):
        q = q_ref[0]
        lane = lax.broadcasted_iota(jnp.int32, q.shape, 1)
        zero = jnp.zeros_like(q)
        kblk = k_ref[0]
        s_half = [_dot_nt(kblk, jnp.where(lane < A_DH, q, zero)),
                  _dot_nt(kblk, jnp.where(lane >= A_DH, q, zero))]
        qmb = qmat.astype(BF16)
        sc = jnp.concatenate([_dot_nt(qmb, kbuf_ref[slot, j].astype(BF16))
                              for j in range(pages_per_step)], axis=1)
        if masked:
            causal = (lax.broadcasted_iota(jnp.int32, (tk, tq), 0)
                      <= lax.broadcasted_iota(jnp.int32, (tk, tq), 1) + (qb * tq - kv * tk))
        tv = vt_ref.shape[-1]
        ones_rows = jnp.ones((acc_ref.shape[0] - A_HD, tv), BF16)
        for half in range(2):
            cols = slice(half * tq, (half + 1) * tq)
            s = jnp.where(causal, s_half[half], -jnp.inf) if masked else s_half[half]
            m_prev = m_ref[:, cols]
            m_new = jnp.maximum(m_prev, jnp.max(s, axis=0, keepdims=True))
            alpha = jnp.exp2(m_prev - m_new)
            m_ref[:, cols] = m_new
            acc = alpha * acc_ref[:, cols]
            for j in range(vt_ref.shape[1]):
                pb = jnp.exp2(s[j * tv:(j + 1) * tv] - m_new).astype(BF16)
                vt_ext = jnp.concatenate([vt_ref[0, j, 0], ones_rows], axis=0)
                acc = acc + _dot(vt_ext, pb)
            acc_ref[:, cols] = acc
        row = lax.broadcasted_iota(jnp.int32, sc.shape, 0)
        col = lax.broadcasted_iota(jnp.int32, sc.shape, 1)
        sc = jnp.where(col % A_HEADS == row % A_HEADS, sc, -jnp.inf)
        sm_prev = sm_ref[...]
        sl_prev = sl_ref[...]
        sacc_prev = sacc_ref[...]
        sm_new = jnp.maximum(sm_prev, jnp.max(sc, axis=1, keepdims=True))
        salpha = jnp.exp2(sm_prev - sm_new)
        sp = jnp.exp2(sc - sm_new)
        spb = sp.astype(BF16)
        pv = _dot(spb[:, 0:page_rows], vbuf_ref[slot, 0].astype(BF16))
        for j in range(1, pages_per_step):
            pv = pv + _dot(spb[:, j * page_rows:(j + 1) * page_rows], vbuf_ref[slot, j].astype(BF16))
        sl_ref[...] = jnp.where(active, salpha * sl_prev + jnp.sum(sp, axis=1, keepdims=True), sl_prev)
        sacc_ref[...] = jnp.where(active, salpha * sacc_prev + pv, sacc_prev)
        sm_ref[...] = jnp.where(active, sm_new, sm_prev)

    @pl.when(kv < kv_last)
    def _():
        dual_step(False)

    @pl.when(kv == kv_last)
    def _():
        dual_step(True)
        o = acc_ref[0:A_HD, :] / acc_ref[A_HD:A_HD + 1, :]
        out = (o[:, :tq] - lam * o[:, tq:]).T
        o_ref[0] = _rms_rows(out, subln) * (1.0 - lam_init)

    @pl.when(jnp.logical_and(active, chunk == chunks_per_seq - 1))
    def _():
        o = sacc_ref[...] / sl_ref[...]
        out = o[:A_HEADS] - lam * o[A_HEADS:]
        os_ref[0] = _rms_rows(out, subln) * (1.0 - lam_init)


def _dual_attention(q, kb, vt, q_s, k_s, v_s, cache_k, cache_v, page_table, lamp, subln,
                    n_seq, seq, lam_init):
    tv = vt.shape[-1]
    tq = min(ATTN_BLOCK, seq)
    tk = min(ATTN_KEY_BLOCK, seq)
    assert seq % tk == 0 and tk % tq == 0 and tq % tv == 0
    nq = seq // tq
    n_dec, n_pages = page_table.shape
    n_pool = cache_k.shape[0]
    pairs = [(n, h, i, j) for n in range(n_seq) for h in range(A_HEADS)
             for i in range(nq) for j in range(((i + 1) * tq - 1) // tk + 1)]
    steps = len(pairs)
    fits = [p for p in range(1, n_pages + 1) if n_pages % p == 0 and n_dec * (n_pages // p) <= steps]
    assert fits, "prompt attention grid too short to carry the cache page stream"
    pps = fits[0]
    cps = n_pages // pps
    paged_steps = n_dec * cps
    tabs = [jnp.asarray([p[c] for p in pairs], jnp.int32) for c in range(4)]

    q3 = q.reshape(n_seq, seq, A_WIDTH)
    k3 = kb.reshape(n_seq, seq, A_WIDTH)
    ck = cache_k.reshape(n_pool, PAGE_SIZE * A_HEADS, A_HD)
    cv = cache_v.reshape(n_pool, PAGE_SIZE * A_HEADS, A_HD)
    stack2 = lambda a: jnp.tile(a.reshape(n_dec, A_HEADS, A_HD), (1, 2, 1))
    rows = 2 * A_HEADS

    pstep = lambda s: jnp.minimum(s, paged_steps - 1)
    const = lambda shape: pl.BlockSpec(shape, lambda s, *_: (0,) * len(shape))
    q_spec = pl.BlockSpec((1, tq, A_HD), lambda s, pt, tn, th, tqb, tkv: (tn[s], tqb[s], th[s]))
    k_spec = pl.BlockSpec((1, tk, A_HD), lambda s, pt, tn, th, tqb, tkv: (tn[s], tkv[s], th[s]))
    vt_spec = pl.BlockSpec((1, tk // tv, 1, A_HD, tv),
                           lambda s, pt, tn, th, tqb, tkv: (tn[s], tkv[s], th[s], 0, 0))
    row_spec = pl.BlockSpec((1, rows, A_HD), lambda s, *_: (pstep(s) // cps, 0, 0))
    hbm_spec = pl.BlockSpec(memory_space=pl.ANY)
    page_buf = pltpu.VMEM((PAGE_RING, pps, PAGE_SIZE * A_HEADS, A_HD), F32)

    grid_spec = pltpu.PrefetchScalarGridSpec(
        num_scalar_prefetch=5,
        grid=(steps,),
        in_specs=[const(lamp.shape), const(subln.shape), q_spec, k_spec, vt_spec,
                  row_spec, row_spec, row_spec, hbm_spec, hbm_spec],
        out_specs=(q_spec, pl.BlockSpec((1, A_HEADS, A_HD), lambda s, *_: (pstep(s) // cps, 0, 0))),
        scratch_shapes=[pltpu.VMEM((1, 2 * tq), F32),
                        pltpu.VMEM((A_HD + DENOM_ROWS, 2 * tq), F32),
                        pltpu.VMEM((rows, 1), F32), pltpu.VMEM((rows, 1), F32),
                        pltpu.VMEM((rows, A_HD), F32),
                        page_buf, page_buf,
                        pltpu.SemaphoreType.DMA((PAGE_RING,)), pltpu.SemaphoreType.DMA((PAGE_RING,))],
    )
    o_p, o_s = pl.pallas_call(
        functools.partial(_dual_attn_kernel, pages_per_step=pps, paged_steps=paged_steps,
                          chunks_per_seq=cps, lam_init=lam_init),
        grid_spec=grid_spec,
        out_shape=(jax.ShapeDtypeStruct((n_seq, seq, A_WIDTH), F32),
                   jax.ShapeDtypeStruct((n_dec, A_HEADS, A_HD), F32)),
        compiler_params=pltpu.CompilerParams(dimension_semantics=("arbitrary",),
                                             vmem_limit_bytes=VMEM_LIMIT_BYTES),
        name="dual_attention",
    )(page_table.reshape(-1), *tabs, lamp, subln, q3, k3, vt, stack2(q_s), stack2(k_s), stack2(v_s), ck, cv)
    return o_p.reshape(n_seq * seq, A_WIDTH), o_s.reshape(n_dec, A_WIDTH)


def _gdn_chunk_kernel(gq_ref, gk_ref, gv_ref, bg_ref, onorm_ref, o_ref, s_out_ref, s_ref):
    tb = pl.program_id(0)
    n_seq, tt, _ = gq_ref.shape
    nc = tt // CHUNK

    @pl.when(tb == 0)
    def _():
        s_ref[...] = jnp.zeros_like(s_ref)

    ii = lax.broadcasted_iota(jnp.int32, (CHUNK, CHUNK), 0)
    jj = lax.broadcasted_iota(jnp.int32, (CHUNK, CHUNK), 1)
    incl = ii >= jj
    strict = ii > jj
    ltri = incl.astype(BF16)
    onorm = onorm_ref[...]

    heads = [(n, h) for n in range(n_seq) for h in range(B_HEADS)]
    n_heads = len(heads)
    group = GDN_CHUNK_GROUP if nc % GDN_CHUNK_GROUP == 0 else 1
    chains = [(cc, n, h) for cc in range(group) for n, h in heads]

    def chunk_group_step(gi, carry):
        rows = [pl.ds(pl.multiple_of((gi * group + cc) * CHUNK, CHUNK), CHUNK) for cc in range(group)]
        head_cols = lambda ref: jnp.stack([ref[n, rows[cc], h * B_DK:(h + 1) * B_DK] for cc, n, h in chains])
        q = head_cols(gq_ref)
        k = head_cols(gk_ref)
        v = head_cols(gv_ref)
        bgc = {(cc, n): bg_ref[n, rows[cc], :] for cc in range(group) for n in range(n_seq)}
        gcum = {key: _dot_exact_lhs(ltri, b) for key, b in bgc.items()}
        lane_bcast = lambda x, c: jnp.broadcast_to(x[:, c:c + 1], (CHUNK, B_DK))
        bh = jnp.stack([lane_bcast(bgc[cc, n], h) for cc, n, h in chains])
        gcs = [lane_bcast(gcum[cc, n], B_HEADS + h) for cc, n, h in chains]
        gc = jnp.stack(gcs)
        d = gc[:, :, :CHUNK] - jnp.stack([g.T[:CHUNK, :] for g in gcs])
        decay = jnp.exp(jnp.where(incl[None], d, -jnp.inf))
        glast = gc[:, CHUNK - 1:CHUNK, :]
        eg = jnp.exp(gc)
        kb = k * bh
        kbf = k.astype(BF16)
        m = -jnp.where(strict[None], _bmm_nt(kb.astype(BF16), kbf) * decay, 0.0)
        sol = jnp.concatenate([v * bh, kb * eg], axis=2)
        mm = lambda level, a, b: (_bmm_split(a, b) if level < SOLVE_SPLIT_LEVELS
                                  else _bmm(a.astype(BF16), b.astype(BF16)))
        sol = sol + mm(0, m, sol)
        p = m
        for level in range(5):
            p = mm(level, p, p)
            sol = sol + mm(level + 1, p, sol)
        u = sol[:, :, :B_DV]
        wb = sol[:, :, B_DV:].astype(BF16)
        qk = jnp.where(incl[None], _bmm_nt(q.astype(BF16), kbf) * decay, 0.0).astype(BF16)
        qg = (q * eg).astype(BF16)
        kg = (k * jnp.exp(glast - gc)).astype(BF16)
        gl = jnp.exp(glast)
        for cc in range(group):
            part = slice(cc * n_heads, (cc + 1) * n_heads)
            s = s_ref[...]
            sb = s.astype(BF16)
            vn = (u[part] - _bmm(wb[part], sb)).astype(BF16)
            o = _bmm(qg[part], sb) + _bmm(qk[part], vn)
            s_ref[...] = s * gl[part] + _bmm_tn(kg[part], vn)
            on = _rms_rows(o, onorm)
            for i, (n, h) in enumerate(heads):
                o_ref[n, rows[cc], h * B_DV:(h + 1) * B_DV] = on[i]
        return carry

    lax.fori_loop(0, nc // group, chunk_group_step, 0)

    @pl.when(tb == pl.num_programs(0) - 1)
    def _():
        s_out_ref[...] = s_ref[...]


def _gdn_chunked(gq, gk, gv, bg, onorm, n_seq, seq):
    tt = min(GDN_ROWS, seq)
    assert seq % tt == 0 and tt % CHUNK == 0
    blk = lambda width: pl.BlockSpec((n_seq, tt, width), lambda t: (0, t, 0))
    state_shape = (n_seq * B_HEADS, B_DK, B_DV)
    out_shape = (jax.ShapeDtypeStruct((n_seq, seq, B_WIDTH), F32),
                 jax.ShapeDtypeStruct(state_shape, F32))
    o, s = pl.pallas_call(
        _gdn_chunk_kernel,
        grid=(seq // tt,),
        in_specs=[blk(B_QK), blk(B_QK), blk(B_WIDTH), blk(LANES), _full(onorm.shape)],
        out_specs=(blk(B_WIDTH), _full(state_shape)),
        out_shape=out_shape,
        scratch_shapes=[pltpu.VMEM(state_shape, F32)],
        compiler_params=pltpu.CompilerParams(dimension_semantics=("arbitrary",),
                                             vmem_limit_bytes=VMEM_LIMIT_BYTES),
        name="gdn_chunked",
    )(gq.reshape(n_seq, seq, B_QK), gk.reshape(n_seq, seq, B_QK), gv.reshape(n_seq, seq, B_WIDTH),
      bg.reshape(n_seq, seq, LANES), onorm)
    return o.reshape(n_seq * seq, B_WIDTH), s.reshape(n_seq, B_HEADS, B_DK, B_DV)


def _gdn_step_kernel(gq_ref, gk_ref, gv_ref, bg_ref, onorm_ref, s_ref, o_ref, so_ref):
    ri = lax.broadcasted_iota(jnp.int32, (B_DK, B_DK), 0)
    ci = lax.broadcasted_iota(jnp.int32, (B_DK, B_DK), 1)
    eye = (ri == ci).astype(F32)
    onorm = onorm_ref[...]
    for b in range(gq_ref.shape[0]):
        bg = bg_ref[b]
        for h in range(B_HEADS):
            sl = slice(h * B_DK, (h + 1) * B_DK)
            q = gq_ref[b][:, sl]
            k = gk_ref[b][:, sl]
            v = gv_ref[b][:, sl]
            beta = bg[:, h:h + 1]
            a = jnp.exp(bg[:, B_HEADS + h:B_HEADS + h + 1])
            kcol = jnp.sum(eye * k, axis=1, keepdims=True)
            qcol = jnp.sum(eye * q, axis=1, keepdims=True)
            s = s_ref[b, h] * a
            u = beta * (v - jnp.sum(s * kcol, axis=0, keepdims=True))
            s = s + kcol * u
            so_ref[b, h] = s
            o = jnp.sum(s * qcol, axis=0, keepdims=True)
            o_ref[b, :, sl] = _rms_rows(o, onorm)


def _gdn_step(gq, gk, gv, bg, onorm, state):
    n_dec = state.shape[0]
    bb = max(p for p in range(1, GDN_STEP_SEQS + 1) if n_dec % p == 0)
    row_spec = lambda width: pl.BlockSpec((bb, 1, width), lambda i: (i, 0, 0))
    s_spec = pl.BlockSpec((bb, B_HEADS, B_DK, B_DV), lambda i: (i, 0, 0, 0))
    out_shape = (jax.ShapeDtypeStruct((n_dec, 1, B_WIDTH), F32),
                 jax.ShapeDtypeStruct(state.shape, F32))
    o, s_new = pl.pallas_call(
        _gdn_step_kernel,
        grid=(n_dec // bb,),
        in_specs=[row_spec(B_QK), row_spec(B_QK), row_spec(B_WIDTH), row_spec(LANES),
                  pl.BlockSpec(onorm.shape, lambda i: (0, 0)), s_spec],
        out_specs=(row_spec(B_WIDTH), s_spec),
        out_shape=out_shape,
        compiler_params=pltpu.CompilerParams(dimension_semantics=("arbitrary",),
                                             vmem_limit_bytes=VMEM_LIMIT_BYTES),
        name="gdn_step",
    )(gq.reshape(n_dec, 1, B_QK), gk.reshape(n_dec, 1, B_QK), gv.reshape(n_dec, 1, B_WIDTH),
      bg.reshape(n_dec, 1, LANES), onorm, state.astype(F32))
    return o.reshape(n_dec, B_WIDTH), s_new


def _merge_out_kernel(x_ref, wn_ref, wg_ref, oa_ref, ob_ref, wua_ref, wub_ref, wo_ref, y_ref):
    x = x_ref[...]
    hb = _rms_rows(x, wn_ref[...]).astype(BF16)
    az = _dot(hb, wg_ref[:, 0:A_WIDTH])
    bz = _dot(hb, wg_ref[:, A_WIDTH:A_WIDTH + B_WIDTH])
    oa = oa_ref[...] * (az * _sigmoid(az))
    ob = ob_ref[...] * (bz * _sigmoid(bz))
    d = x.shape[1]
    g0 = A_WIDTH + B_WIDTH
    ga = _dot(hb, wg_ref[:, g0:g0 + d])
    gb = _dot(hb, wg_ref[:, g0 + d:g0 + 2 * d])
    y = (_sigmoid(ga) * _dot(oa.astype(BF16), wua_ref[...])
         + _sigmoid(gb) * _dot(ob.astype(BF16), wub_ref[...]))
    y_ref[...] = x + _dot(y.astype(BF16), wo_ref[...])


def _merge_out(x2, wn, w_gate, o_a, o_b, w_up_a, w_up_b, w_out, tm):
    rows, d = x2.shape
    assert rows % tm == 0
    row_spec = lambda width: pl.BlockSpec((tm, width), lambda i: (i, 0))
    return pl.pallas_call(
        _merge_out_kernel,
        grid=(rows // tm,),
        in_specs=[row_spec(d), _full(wn.shape), _full(w_gate.shape), row_spec(A_WIDTH), row_spec(B_WIDTH),
                  _full(w_up_a.shape), _full(w_up_b.shape), _full(w_out.shape)],
        out_specs=row_spec(d),
        out_shape=jax.ShapeDtypeStruct((rows, d), F32),
        compiler_params=pltpu.CompilerParams(dimension_semantics=("arbitrary",),
                                             vmem_limit_bytes=VMEM_LIMIT_BYTES),
        name="merge_out",
    )(x2, wn, w_gate, o_a, o_b, w_up_a, w_up_b, w_out)


def _rope_tables(pos):
    half = A_DH // 2
    inv = ROPE_THETA ** (-jnp.arange(half, dtype=F32) / half)
    ang = pos.astype(F32)[:, None] * inv[None, :]
    cos = jnp.cos(ang)
    sin = jnp.sin(ang)
    return jnp.tile(cos, (1, 4)), jnp.concatenate([-sin, sin, -sin, sin], axis=1)


def _repack_w_in(w):
    d = w.shape[0]
    o = 0
    parts = {}
    for name, size in (("aq", A_WIDTH), ("ak", A_WIDTH), ("av", A_WIDTH), ("az", A_WIDTH),
                       ("bq", B_QK), ("bk", B_QK), ("bv", B_WIDTH), ("bz", B_WIDTH),
                       ("bb", B_HEADS), ("ba", B_HEADS), ("ga", d), ("gb", d)):
        parts[name] = w[:, o:o + size]
        o += size
    assert o == w.shape[1]
    pad = jnp.zeros((d, LANES - 2 * B_HEADS), w.dtype)
    w_proj = jnp.concatenate([parts[n] for n in ("aq", "ak", "av", "bq", "bk", "bv", "bb", "ba")] + [pad], axis=1)
    w_gate = jnp.concatenate([parts[n] for n in ("az", "bz", "ga", "gb")], axis=1)
    return w_proj.astype(BF16), w_gate.astype(BF16)


def _lane_row(vec, offset):
    return jnp.zeros((1, LANES), F32).at[0, offset:offset + vec.shape[0]].set(vec.astype(F32))


def kernel(x_prompt, x_sample, cache_k, cache_v, state_conv, state_ssm, page_table, w_norm, w_in, a_qn, a_kn, a_lq1, a_lk1, a_lq2, a_lk2, a_subln, conv_w, a_log, dt_bias, b_onorm, w_up_a, w_up_b, w_out):
    depth = w_norm.shape[0]
    n_seq, seq, d = x_prompt.shape
    n_dec, dec_seq, _ = x_sample.shape
    assert dec_seq == 1, "the sample path handles one new token per sequence"
    n_pages = page_table.shape[1]
    past_len = n_pages * PAGE_SIZE

    cos_p, sin_p = _rope_tables(jnp.arange(seq, dtype=jnp.int32))
    cos_s, sin_s = _rope_tables(past_len + jnp.arange(dec_seq, dtype=jnp.int32))
    sub = lax.broadcasted_iota(jnp.int32, (A_WIDTH, A_WIDTH), 0) // A_DH
    gsum = jnp.where(sub == sub.T, 1.0 / A_DH, 0.0).astype(BF16)

    hp = x_prompt.reshape(n_seq * seq, d)
    hs = x_sample.reshape(n_dec, d)
    outs = [[] for _ in range(8)]
    for l in range(depth):
        lam_init = 0.8 - 0.6 * math.exp(-0.3 * l)
        wn = w_norm[l].reshape(1, d)
        w_proj, w_gate = _repack_w_in(w_in[l])
        qn = jnp.tile(a_qn[l], A_WIDTH // A_DH).reshape(1, A_WIDTH)
        kn = jnp.tile(a_kn[l], A_WIDTH // A_DH).reshape(1, A_WIDTH)
        lamp = jnp.stack([a_lq1[l], a_lk1[l], a_lq2[l], a_lk2[l]]).astype(F32)
        subln = a_subln[l].reshape(1, A_HD)
        alog = _lane_row(a_log[l], B_HEADS)
        dtb = _lane_row(dt_bias[l], B_HEADS)
        onorm = b_onorm[l].reshape(1, B_DV)
        wua = w_up_a[l].astype(BF16)
        wub = w_up_b[l].astype(BF16)
        wo = w_out[l].astype(BF16)
        convw = conv_w[l]

        q, k, kb, v, vt, gq, gk, gv, bg, conv_p = _prompt_projection(
            hp, n_seq, seq, wn, w_proj, gsum, qn, kn, cos_p, sin_p, convw, alog, dtb)
        sc = state_conv[l]
        qs, ks, vs, gqs, gks, gvs, bgs, us = _sample_projection(
            hs, wn, w_proj, gsum, qn, kn, cos_s, sin_s, convw, alog, dtb,
            sc[:, 0, :], sc[:, 1, :], sc[:, 2, :])
        o_a, o_as = _dual_attention(q, kb, vt, qs, ks, vs, cache_k[l], cache_v[l], page_table,
                                    lamp, subln, n_seq, seq, lam_init)
        o_b, ssm_p = _gdn_chunked(gq, gk, gv, bg, onorm, n_seq, seq)
        hp = _merge_out(hp, wn, w_gate, o_a, o_b, wua, wub, wo, min(MERGE_ROWS, seq))
        o_bs, ssm_s = _gdn_step(gqs, gks, gvs, bgs, onorm, state_ssm[l])
        hs = _merge_out(hs, wn, w_gate, o_as, o_bs, wua, wub, wo, n_dec)

        outs[0].append(k.reshape(n_seq, seq, A_HEADS, A_HD))
        outs[1].append(v.reshape(n_seq, seq, A_HEADS, A_HD))
        outs[2].append(conv_p)
        outs[3].append(ssm_p.astype(x_prompt.dtype))
        outs[4].append(ks.reshape(n_dec, dec_seq, A_HEADS, A_HD))
        outs[5].append(vs.reshape(n_dec, dec_seq, A_HEADS, A_HD))
        outs[6].append(jnp.concatenate([sc[:, 1:, :], us[:, None, :]], axis=1))
        outs[7].append(ssm_s.astype(x_sample.dtype))

    return (hp.reshape(n_seq, seq, d), hs.reshape(n_dec, dec_seq, d)) + tuple(jnp.stack(o) for o in outs)
```
